```python
import math
import jax
import jax.numpy as jnp
from jax import lax
import numpy as np

D_MODEL = 1024
BATCH = 1
SEQ = 16384
DEPTH = 2
DEC_BATCH = 128
DEC_SEQ = 4
PAST_LEN = 16384
PAGE_SIZE = 128

CONF_W = 512
CONF_K = 31
DIFF_H = 4
DIFF_DK = 64
DIFF_DV = 2 * DIFF_DK
DIFF_W = DIFF_H * DIFF_DV
SCONV_W = 512
SCONV_K = 3
MLA_H = 8
MLA_NOPE = 64
MLA_ROPE = 32
MLA_DV = 64
MLA_W = MLA_H * MLA_DV
Q_LORA = 256
KV_LORA = 256

ROPE_THETA = 10000.0
NORM_EPS = 1e-6
Q_BLOCK = 128
N_EVEN = (DEPTH + 1) // 2
N_ODD = DEPTH // 2

EVEN_SPLITS = (CONF_W, CONF_W, CONF_W, DIFF_H * 2 * DIFF_DK, DIFF_H * 2 * DIFF_DK, DIFF_W, DIFF_W)
ODD_SPLITS = (SCONV_W, SCONV_W, SCONV_W, SCONV_W, Q_LORA, KV_LORA, MLA_ROPE, MLA_W)
EVEN_IN = sum(EVEN_SPLITS)
ODD_IN = sum(ODD_SPLITS)
EVEN_MIX = CONF_W + DIFF_W
ODD_MIX = SCONV_W + MLA_W

F32 = jnp.float32

kernel_name = 'hybrid_conformer_diffattn_shortconv_mla_step'


def _split(x, sizes):
    idx = [int(v) for v in np.cumsum(sizes)[:-1]]
    return jnp.split(x, idx, axis=-1)


def _rms_norm(x, g):
    xf = x.astype(F32)
    y = xf * lax.rsqrt(jnp.mean(xf * xf, axis=-1, keepdims=True) + NORM_EPS)
    return (y * g.astype(F32)).astype(x.dtype)


def _layer_norm(x, g, b):
    xf = x.astype(F32)
    mu = jnp.mean(xf, axis=-1, keepdims=True)
    var = jnp.mean(jnp.square(xf - mu), axis=-1, keepdims=True)
    y = (xf - mu) * lax.rsqrt(var + NORM_EPS) * g.astype(F32) + b.astype(F32)
    return y.astype(x.dtype)


def _rope(x, pos):
    d = x.shape[-1]
    inv = ROPE_THETA ** (-jnp.arange(0, d, 2, dtype=F32) / d)
    ang = pos.astype(F32)[:, None] * inv[None, :]
    cos = jnp.cos(ang)[:, None, :]
    sin = jnp.sin(ang)[:, None, :]
    xf = x.astype(F32)
    x1, x2 = xf[..., : d // 2], xf[..., d // 2:]
    return jnp.concatenate([x1 * cos - x2 * sin, x2 * cos + x1 * sin], axis=-1).astype(x.dtype)


def _causal_dwconv(u, w, prev):
    full = jnp.concatenate([prev.astype(u.dtype), u], axis=1)
    out = lax.conv_general_dilated(
        full, w.astype(u.dtype)[:, None, :], window_strides=(1,), padding='VALID',
        dimension_numbers=('NWC', 'WIO', 'NWC'), feature_group_count=u.shape[-1])
    return out, full[:, -(w.shape[0] - 1):]


def _online(carry, s, v, eq):
    m, l, acc = carry
    m_new = jnp.maximum(m, jnp.max(s, axis=-1))
    a = jnp.exp(m - m_new)
    p = jnp.exp(s - m_new[..., None])
    acc = acc * a[..., None] + jnp.einsum(eq, p, v.astype(F32))
    return (m_new, l * a + jnp.sum(p, axis=-1), acc)


def _causal_mask_block(i, S):
    qpos = i * Q_BLOCK + jnp.arange(Q_BLOCK)
    return jnp.arange(S)[None, :] <= qpos[:, None]


def _diff_attn_prompt(q, k, v, lam):
    B, S = q.shape[:2]
    nb = S // Q_BLOCK
    scale = DIFF_DK ** -0.5
    qb = jnp.moveaxis(q.reshape(B, nb, Q_BLOCK, DIFF_H, 2, DIFF_DK), 1, 0)

    def block(args):
        qi, i = args
        s = jnp.einsum('bqhmd,bkhmd->bmhqk', qi, k, preferred_element_type=F32) * scale
        s = jnp.where(_causal_mask_block(i, S), s, -jnp.inf)
        p = jax.nn.softmax(s, axis=-1)
        pd = (p[:, 0] - lam * p[:, 1]).astype(v.dtype)
        return jnp.einsum('bhqk,bkhd->bqhd', pd, v)

    o = lax.map(block, (qb, jnp.arange(nb)))
    return jnp.moveaxis(o, 0, 1).reshape(B, S, DIFF_H, DIFF_DV)


def _diff_attn_sample(q, k_new, v_new, lam, page_table, pool_k, pool_v, li):
    Bd, T = q.shape[:2]
    scale = DIFF_DK ** -0.5
    eq = 'bmhqk,bkhd->bmhqd'
    carry = (jnp.full((Bd, 2, DIFF_H, T), -jnp.inf, F32),
             jnp.zeros((Bd, 2, DIFF_H, T), F32),
             jnp.zeros((Bd, 2, DIFF_H, T, DIFF_DV), F32))

    def page_step(c, ids):
        kb = pool_k[li, ids]
        vb = pool_v[li, ids]
        s = jnp.einsum('bqhmd,bkhmd->bmhqk', q, kb.astype(q.dtype), preferred_element_type=F32) * scale
        return _online(c, s, vb, eq), None

    carry, _ = lax.scan(page_step, carry, page_table.T)
    s = jnp.einsum('bqhmd,bkhmd->bmhqk', q, k_new, preferred_element_type=F32) * scale
    s = jnp.where(jnp.tril(jnp.ones((T, T), bool)), s, -jnp.inf)
    _, l, acc = _online(carry, s, v_new, eq)
    o = acc / l[..., None]
    y = o[:, 0] - lam * o[:, 1]
    return jnp.transpose(y, (0, 2, 1, 3)).astype(q.dtype)


def _mla_prompt(q_abs, q_pe, ckv, kpe):
    B, S = q_abs.shape[:2]
    nb = S // Q_BLOCK
    scale = (MLA_NOPE + MLA_ROPE) ** -0.5
    qa = jnp.moveaxis(q_abs.reshape(B, nb, Q_BLOCK, MLA_H, KV_LORA), 1, 0)
    qp = jnp.moveaxis(q_pe.reshape(B, nb, Q_BLOCK, MLA_H, MLA_ROPE), 1, 0)

    def block(args):
        qai, qpi, i = args
        s = (jnp.einsum('bqhr,bkr->bhqk', qai, ckv, preferred_element_type=F32)
             + jnp.einsum('bqhe,bke->bhqk', qpi, kpe, preferred_element_type=F32)) * scale
        s = jnp.where(_causal_mask_block(i, S), s, -jnp.inf)
        p = jax.nn.softmax(s, axis=-1).astype(ckv.dtype)
        return jnp.einsum('bhqk,bkr->bqhr', p, ckv)

    o = lax.map(block, (qa, qp, jnp.arange(nb)))
    return jnp.moveaxis(o, 0, 1).reshape(B, S, MLA_H, KV_LORA)


def _mla_sample(q_abs, q_pe, ckv_new, kpe_new, page_table, pool_ckv, pool_kpe, li):
    Bd, T = q_abs.shape[:2]
    scale = (MLA_NOPE + MLA_ROPE) ** -0.5
    eq = 'bhqk,bkr->bhqr'
    carry = (jnp.full((Bd, MLA_H, T), -jnp.inf, F32),
             jnp.zeros((Bd, MLA_H, T), F32),
             jnp.zeros((Bd, MLA_H, T, KV_LORA), F32))

    def scores(cb, eb):
        return (jnp.einsum('bqhr,bkr->bhqk', q_abs, cb.astype(q_abs.dtype), preferred_element_type=F32)
                + jnp.einsum('bqhe,bke->bhqk', q_pe, eb.astype(q_pe.dtype), preferred_element_type=F32)) * scale

    def page_step(c, ids):
        cb = pool_ckv[li, ids]
        eb = pool_kpe[li, ids]
        return _online(c, scores(cb, eb), cb, eq), None

    carry, _ = lax.scan(page_step, carry, page_table.T)
    s = jnp.where(jnp.tril(jnp.ones((T, T), bool)), scores(ckv_new, kpe_new), -jnp.inf)
    _, l, acc = _online(carry, s, ckv_new, eq)
    o = acc / l[..., None]
    return jnp.transpose(o, (0, 2, 1, 3)).astype(q_abs.dtype)


def _even_layer(x, pos, conf_prev, paged, lam_init, norm_g, w_in, dw_w, dw_b, ln_g, ln_b,
                lq1, lk1, lq2, lk2, subln_g, w_out):
    B, T, _ = x.shape
    h = _rms_norm(x, norm_g)
    gv, gg, za, q, k, v, zb = _split(h @ w_in, EVEN_SPLITS)
    u = gv * jax.nn.sigmoid(gg)
    c, conf_state = _causal_dwconv(u, dw_w, conf_prev)
    a_out = jax.nn.silu(_layer_norm(c + dw_b, ln_g, ln_b))
    q = _rope(q.reshape(B, T, DIFF_H * 2, DIFF_DK), pos).reshape(B, T, DIFF_H, 2, DIFF_DK)
    k = _rope(k.reshape(B, T, DIFF_H * 2, DIFF_DK), pos).reshape(B, T, DIFF_H, 2, DIFF_DK)
    v = v.reshape(B, T, DIFF_H, DIFF_DV)
    lam = (jnp.exp(jnp.sum(lq1.astype(F32) * lk1.astype(F32)))
           - jnp.exp(jnp.sum(lq2.astype(F32) * lk2.astype(F32))) + lam_init)
    if paged is None:
        o = _diff_attn_prompt(q, k, v, lam)
    else:
        o = _diff_attn_sample(q, k, v, lam, *paged)
    o = _rms_norm(o, subln_g) * (1.0 - lam_init)
    b_out = o.reshape(B, T, DIFF_W)
    mix = jnp.concatenate([jax.nn.silu(za) * a_out, jax.nn.silu(zb) * b_out], axis=-1) @ w_out
    return x + mix, (k, v, conf_state)


def _odd_layer(x, pos, sconv_prev, paged, norm_g, w_in, sconv_w, q_norm_g, w_uq, kv_norm_g,
               w_uk, w_uv, w_out):
    B, T, _ = x.shape
    h = _rms_norm(x, norm_g)
    hc, bc, cc, zc, cq, ckv, kr, zd = _split(h @ w_in, ODD_SPLITS)
    conv, sconv_state = _causal_dwconv(cc * hc, sconv_w, sconv_prev)
    c_out = jax.nn.silu(zc) * (bc * conv)
    q = (_rms_norm(cq, q_norm_g) @ w_uq).reshape(B, T, MLA_H, MLA_NOPE + MLA_ROPE)
    q_nope = q[..., :MLA_NOPE]
    q_pe = _rope(q[..., MLA_NOPE:], pos)
    ckv = _rms_norm(ckv, kv_norm_g)
    kpe = _rope(kr[:, :, None, :], pos)[:, :, 0]
    q_abs = jnp.einsum('bthn,rhn->bthr', q_nope, w_uk)
    if paged is None:
        o = _mla_prompt(q_abs, q_pe, ckv, kpe)
    else:
        o = _mla_sample(q_abs, q_pe, ckv, kpe, *paged)
    d_out = jnp.einsum('bthr,rhd->bthd', o, w_uv).reshape(B, T, MLA_W)
    mix = jnp.concatenate([c_out, jax.nn.silu(zd) * d_out], axis=-1) @ w_out
    return x + mix, (ckv, kpe, sconv_state)


def setup_inputs(seed: int = 0) -> dict:
    key = jax.random.key(seed)
    ks = iter(jax.random.split(key, 32))

    def nrm(shape, scale=1.0):
        return jax.random.normal(next(ks), shape, F32) * scale

    n_pages = PAST_LEN // PAGE_SIZE
    n_pool = (DEC_BATCH * n_pages * 5) // 4
    page_table = jax.random.permutation(next(ks), n_pool)[: DEC_BATCH * n_pages]
    page_table = page_table.reshape(DEC_BATCH, n_pages).astype(jnp.int32)
    return {
        'x_prompt': nrm((BATCH, SEQ, D_MODEL)),
        'x_sample': nrm((DEC_BATCH, DEC_SEQ, D_MODEL)),
        'cache_diff_k': nrm((N_EVEN, n_pool, PAGE_SIZE, DIFF_H, 2, DIFF_DK)),
        'cache_diff_v': nrm((N_EVEN, n_pool, PAGE_SIZE, DIFF_H, DIFF_DV)),
        'state_conf': nrm((N_EVEN, DEC_BATCH, CONF_K - 1, CONF_W), 0.5),
        'cache_mla_ckv': nrm((N_ODD, n_pool, PAGE_SIZE, KV_LORA)),
        'cache_mla_krope': nrm((N_ODD, n_pool, PAGE_SIZE, MLA_ROPE)),
        'state_sconv': nrm((N_ODD, DEC_BATCH, SCONV_K - 1, SCONV_W)),
        'page_table': page_table,
        'norm_g': 1.0 + nrm((DEPTH, D_MODEL), 0.02),
        'final_norm_g': 1.0 + nrm((D_MODEL,), 0.02),
        'w_in_even': nrm((N_EVEN, D_MODEL, EVEN_IN), D_MODEL ** -0.5),
        'conf_dw_w': nrm((N_EVEN, CONF_K, CONF_W), CONF_K ** -0.5),
        'conf_dw_b': nrm((N_EVEN, CONF_W), 0.02),
        'conf_ln_g': 1.0 + nrm((N_EVEN, CONF_W), 0.02),
        'conf_ln_b': nrm((N_EVEN, CONF_W), 0.02),
        'lam_q1': nrm((N_EVEN, DIFF_DK), 0.1),
        'lam_k1': nrm((N_EVEN, DIFF_DK), 0.1),
        'lam_q2': nrm((N_EVEN, DIFF_DK), 0.1),
        'lam_k2': nrm((N_EVEN, DIFF_DK), 0.1),
        'diff_subln_g': 1.0 + nrm((N_EVEN, DIFF_DV), 0.02),
        'w_out_even': nrm((N_EVEN, EVEN_MIX, D_MODEL), EVEN_MIX ** -0.5),
        'w_in_odd': nrm((N_ODD, D_MODEL, ODD_IN), D_MODEL ** -0.5),
        'sconv_w': nrm((N_ODD, SCONV_K, SCONV_W), SCONV_K ** -0.5),
        'mla_q_norm_g': 1.0 + nrm((N_ODD, Q_LORA), 0.02),
        'w_uq': nrm((N_ODD, Q_LORA, MLA_H * (MLA_NOPE + MLA_ROPE)), Q_LORA ** -0.5),
        'mla_kv_norm_g': 1.0 + nrm((N_ODD, KV_LORA), 0.02),
        'w_uk': nrm((N_ODD, KV_LORA, MLA_H, MLA_NOPE), KV_LORA ** -0.5),
        'w_uv': nrm((N_ODD, KV_LORA, MLA_H, MLA_DV), KV_LORA ** -0.5),
        'w_out_odd': nrm((N_ODD, ODD_MIX, D_MODEL), ODD_MIX ** -0.5),
    }


def reference(x_prompt, x_sample, cache_diff_k, cache_diff_v, state_conf, cache_mla_ckv,
              cache_mla_krope, state_sconv, page_table, norm_g, final_norm_g, w_in_even,
              conf_dw_w, conf_dw_b, conf_ln_g, conf_ln_b, lam_q1, lam_k1, lam_q2, lam_k2,
              diff_subln_g, w_out_even, w_in_odd, sconv_w, mla_q_norm_g, w_uq, mla_kv_norm_g,
              w_uk, w_uv, w_out_odd):
    B, S, _ = x_prompt.shape
    T = x_sample.shape[1]
    past_len = page_table.shape[1] * PAGE_SIZE
    pos_p = jnp.arange(S)
    pos_s = past_len + jnp.arange(T)
    xp, xs = x_prompt, x_sample
    dk_p, dv_p, cf_p, ck_p, kr_p, sc_p = [], [], [], [], [], []
    dk_s, dv_s, cf_s, ck_s, kr_s, sc_s = [], [], [], [], [], []
    for l in range(DEPTH):
        i = l // 2
        if l % 2 == 0:
            lam_init = 0.8 - 0.6 * math.exp(-0.3 * l)
            w = (norm_g[l], w_in_even[i], conf_dw_w[i], conf_dw_b[i], conf_ln_g[i], conf_ln_b[i],
                 lam_q1[i], lam_k1[i], lam_q2[i], lam_k2[i], diff_subln_g[i], w_out_even[i])
            zero_prev = jnp.zeros((B, CONF_K - 1, CONF_W), xp.dtype)
            xp, (k1, v1, c1) = _even_layer(xp, pos_p, zero_prev, None, lam_init, *w)
            xs, (k2, v2, c2) = _even_layer(xs, pos_s, state_conf[i],
                                           (page_table, cache_diff_k, cache_diff_v, i), lam_init, *w)
            dk_p.append(k1); dv_p.append(v1); cf_p.append(c1)
            dk_s.append(k2); dv_s.append(v2); cf_s.append(c2)
        else:
            w = (norm_g[l], w_in_odd[i], sconv_w[i], mla_q_norm_g[i], w_uq[i], mla_kv_norm_g[i],
                 w_uk[i], w_uv[i], w_out_odd[i])
            zero_prev = jnp.zeros((B, SCONV_K - 1, SCONV_W), xp.dtype)
            xp, (c1, e1, s1) = _odd_layer(xp, pos_p, zero_prev, None, *w)
            xs, (c2, e2, s2) = _odd_layer(xs, pos_s, state_sconv[i],
                                          (page_table, cache_mla_ckv, cache_mla_krope, i), *w)
            ck_p.append(c1); kr_p.append(e1); sc_p.append(s1)
            ck_s.append(c2); kr_s.append(e2); sc_s.append(s2)
    y_prompt = _rms_norm(xp, final_norm_g)
    y_sample = _rms_norm(xs, final_norm_g)
    return (y_prompt, y_sample,
            jnp.stack(dk_p), jnp.stack(dv_p), jnp.stack(cf_p),
            jnp.stack(ck_p), jnp.stack(kr_p), jnp.stack(sc_p),
            jnp.stack(dk_s), jnp.stack(dv_s), jnp.stack(cf_s),
            jnp.stack(ck_s), jnp.stack(kr_s), jnp.stack(sc_s))
```

```python
import functools
import math

import jax
import jax.numpy as jnp
from jax import lax
from jax.experimental import pallas as pl
from jax.experimental.pallas import tpu as pltpu

F32 = jnp.float32
BF16 = jnp.bfloat16

NORM_EPS = 1e-6
ROPE_THETA = 10000.0
LOG2E = 1.4426950408889634
NEG_INIT = -1e30

LANES = 128
BRANCH_W = 512
DIFF_H, DIFF_DK, DIFF_DV = 4, 64, 128
MLA_H, MLA_NOPE, MLA_ROPE, MLA_DV = 8, 64, 32, 64
Q_LORA = KV_LORA = 256
MLA_QSLOT = 128
ROPE_LANE0 = MLA_NOPE
PAGE = 128

CONV_ROWS = 32
VMEM_LIMIT = 56 * 1024 * 1024

_NT = (((1,), (1,)), ((), ()))


def _rms(x, g):
    return x * lax.rsqrt(jnp.mean(x * x, axis=-1, keepdims=True) + NORM_EPS) * g


def _silu(x):
    return x * jax.nn.sigmoid(x)


def _rope128(x, cos, sin, half):
    lane = lax.broadcasted_iota(jnp.int32, x.shape, 1)
    first = (lane & (2 * half - 1)) < half
    up = pltpu.roll(x, LANES - half, 1)
    dn = pltpu.roll(x, half, 1)
    return x * cos + jnp.where(first, up, dn) * sin


def _rope_wide(x, cos, sin, half):
    n = x.shape[1] // LANES
    return jnp.concatenate(
        [_rope128(x[:, c * LANES:(c + 1) * LANES], cos, sin, half) for c in range(n)], axis=1)


def _resident(shape):
    return pl.BlockSpec(shape, lambda *_: (0,) * len(shape), pipeline_mode=pl.Buffered(1))


def _dw_conv_chunk(buf, w_ref, r0, off, dil, taps):
    acc = None
    for j in range(taps):
        a = r0 + off + j * dil
        t = buf[a:a + CONV_ROWS, :] * w_ref[j:j + 1, :]
        acc = t if acc is None else acc + t
    return acc


def _even_front_kernel(x_ref, hist_ref, g_ref, w_ref, cos_ref, sin_ref, dww_ref, dwb_ref,
                       lng_ref, lnb_ref,
                       ag_ref, zb_ref, qb_ref, kf_ref, kb_ref, vf_ref, vb_ref, st_ref,
                       ubuf, *, tm, hp, off, dil, taps, carry, qscale):
    W = BRANCH_W

    @pl.when(pl.program_id(0) == 0)
    def _():
        ubuf[0:hp, :] = hist_ref[...]

    hb = _rms(x_ref[...], g_ref[...]).astype(BF16)

    def proj(c):
        return jnp.dot(hb, w_ref[:, c * W:(c + 1) * W], preferred_element_type=F32)

    ubuf[hp:hp + tm, :] = proj(0) * jax.nn.sigmoid(proj(1))
    za = proj(2)
    for r0 in range(0, tm, CONV_ROWS):
        c = _dw_conv_chunk(ubuf, dww_ref, r0, off, dil, taps) + dwb_ref[...]
        d = c - jnp.mean(c, axis=-1, keepdims=True)
        y = d * lax.rsqrt(jnp.mean(d * d, axis=-1, keepdims=True) + NORM_EPS)
        y = y * lng_ref[...] + lnb_ref[...]
        ag_ref[r0:r0 + CONV_ROWS, :] = (_silu(za[r0:r0 + CONV_ROWS, :]) * _silu(y)).astype(BF16)
    ns = st_ref.shape[0]
    st_ref[...] = ubuf[hp + tm - ns:hp + tm, :]
    if carry:
        ubuf[0:hp, :] = ubuf[tm:tm + hp, :]

    cos = cos_ref[...]
    sin = sin_ref[...]
    q = _rope_wide(proj(3), cos, sin, DIFF_DK // 2)
    qb_ref[...] = (q * qscale).astype(BF16)
    k = _rope_wide(proj(4), cos, sin, DIFF_DK // 2)
    kf_ref[...] = k
    kb_ref[...] = k.astype(BF16)
    v = proj(5)
    vf_ref[...] = v
    vb_ref[...] = v.astype(BF16)
    zb_ref[...] = _silu(proj(6))


def _even_front(x, hist, g, w_bf, cos, sin, dww, dwb, lng, lnb, *, tm, dil):
    rows, d = x.shape
    taps = dww.shape[0]
    hp = hist.shape[0]
    off = hp - (taps - 1) * dil
    W = BRANCH_W
    n = rows // tm
    row = lambda i: (i, 0)
    fix = lambda i: (0, 0)
    kern = functools.partial(_even_front_kernel, tm=tm, hp=hp, off=off, dil=dil, taps=taps,
                             carry=n > 1, qscale=DIFF_DK ** -0.5 * LOG2E)
    wide = lambda dt: jax.ShapeDtypeStruct((rows, W), dt)
    ns = min(hp, tm)
    return pl.pallas_call(
        kern,
        grid=(n,),
        in_specs=[pl.BlockSpec((tm, d), row), _resident((hp, W)),
                  pl.BlockSpec((1, d), fix), _resident(w_bf.shape),
                  pl.BlockSpec((tm, LANES), row), pl.BlockSpec((tm, LANES), row),
                  pl.BlockSpec((taps, W), fix), pl.BlockSpec((1, W), fix),
                  pl.BlockSpec((1, W), fix), pl.BlockSpec((1, W), fix)],
        out_specs=[pl.BlockSpec((tm, W), row)] * 7 + [pl.BlockSpec((ns, W), fix)],
        out_shape=[wide(BF16), wide(F32), wide(BF16), wide(F32), wide(BF16), wide(F32),
                   wide(BF16), jax.ShapeDtypeStruct((ns, W), F32)],
        scratch_shapes=[pltpu.VMEM((hp + tm, W), F32)],
        compiler_params=pltpu.CompilerParams(dimension_semantics=("arbitrary",),
                                             vmem_limit_bytes=VMEM_LIMIT),
        name="even_front",
    )(x, hist, g, w_bf, cos, sin, dww, dwb, lng, lnb)


def _flash_body(q_ref, k_ref, v_ref, m_ref, l_ref, acc_ref, *, tq, tk):
    qi = pl.program_id(1)
    q = q_ref[...]
    dk = q.shape[1]
    lane = lax.broadcasted_iota(jnp.int32, q.shape, 1)
    zero = jnp.zeros_like(q)
    q2 = jnp.concatenate([jnp.where(lane < dk // 2, q, zero),
                          jnp.where(lane >= dk // 2, q, zero)], axis=0)
    m_ref[...] = jnp.full(m_ref.shape, NEG_INIT, F32)
    l_ref[...] = jnp.zeros(l_ref.shape, F32)
    acc_ref[...] = jnp.zeros(acc_ref.shape, F32)

    def step(kb, masked):
        ks = pl.multiple_of(kb * tk, tk)
        s = lax.dot_general(q2, k_ref[pl.ds(ks, tk), :], _NT, preferred_element_type=F32)
        if masked:
            row = qi * tq + lax.rem(lax.broadcasted_iota(jnp.int32, s.shape, 0), tq)
            col = ks + lax.broadcasted_iota(jnp.int32, s.shape, 1)
            s = jnp.where(col <= row, s, -jnp.inf)
        m_prev = m_ref[:, 0:1]
        m_new = jnp.maximum(m_prev, jnp.max(s, axis=-1, keepdims=True))
        alpha = jnp.exp2(m_prev - m_new)
        p = jnp.exp2(s - m_new)
        l_ref[...] = alpha * l_ref[...] + jnp.sum(p, axis=-1, keepdims=True)
        acc_ref[...] = alpha * acc_ref[...] + jnp.dot(
            p.astype(BF16), v_ref[pl.ds(ks, tk), :], preferred_element_type=F32)
        m_ref[...] = jnp.broadcast_to(m_new, m_ref.shape)

    nfull = (qi * tq) // tk

    def loop_body(kb, c):
        step(kb, False)
        return c

    lax.fori_loop(0, nfull, loop_body, 0)
    step(nfull, True)
    o = acc_ref[...] / l_ref[:, 0:1]
    return o[:tq], o[tq:]


def _lam(lq1_ref, lk1_ref, lq2_ref, lk2_ref, lam_init):
    a = jnp.sum(lq1_ref[...] * lk1_ref[...], axis=-1, keepdims=True)
    b = jnp.sum(lq2_ref[...] * lk2_ref[...], axis=-1, keepdims=True)
    return jnp.exp(a) - jnp.exp(b) + lam_init


def _flash_diff_kernel(q_ref, k_ref, v_ref, z_ref, lq1_ref, lk1_ref, lq2_ref, lk2_ref, sg_ref,
                       o_ref, m_ref, l_ref, acc_ref, *, tq, tk, lam_init):
    o0, o1 = _flash_body(q_ref, k_ref, v_ref, m_ref, l_ref, acc_ref, tq=tq, tk=tk)
    o = o0 - _lam(lq1_ref, lk1_ref, lq2_ref, lk2_ref, lam_init) * o1
    o = _rms(o, sg_ref[...]) * (1.0 - lam_init)
    o_ref[...] = (z_ref[...] * o).astype(BF16)


def _flash_mla_kernel(q_ref, k_ref, v_ref, z_ref, o_ref, m_ref, l_ref, acc_ref, *, tq, tk):
    o0, o1 = _flash_body(q_ref, k_ref, v_ref, m_ref, l_ref, acc_ref, tq=tq, tk=tk)
    lane = lax.broadcasted_iota(jnp.int32, o0.shape, 1)
    o_ref[...] = (z_ref[...] * jnp.where(lane < MLA_DV, o0, o1)).astype(BF16)


def _flash(kern, q, k, v, z, extra, *, dk, tq, tk, name):
    S = q.shape[0]
    groups = q.shape[1] // dk
    blk = lambda g, i: (i, g)
    res = lambda g, i: (0, g)
    fix = lambda g, i: (0, 0)
    return pl.pallas_call(
        functools.partial(kern, tq=tq, tk=tk),
        grid=(groups, S // tq),
        in_specs=[pl.BlockSpec((tq, dk), blk), pl.BlockSpec((S, dk), res),
                  pl.BlockSpec((S, LANES), res), pl.BlockSpec((tq, LANES), blk)]
                 + [pl.BlockSpec(e.shape, fix) for e in extra],
        out_specs=pl.BlockSpec((tq, LANES), blk),
        out_shape=jax.ShapeDtypeStruct((S, groups * LANES), BF16),
        scratch_shapes=[pltpu.VMEM((2 * tq, LANES), F32), pltpu.VMEM((2 * tq, LANES), F32),
                        pltpu.VMEM((2 * tq, LANES), F32)],
        compiler_params=pltpu.CompilerParams(dimension_semantics=("arbitrary", "arbitrary"),
                                             vmem_limit_bytes=VMEM_LIMIT),
        name=name,
    )(q, k, v, z, *extra)


def _mid_kernel(x_ref, ag_ref, bg_ref, hist_ref, wo_ref, g_ref, w_ref, scw_ref, qng_ref, wuq_ref,
                kvg_ref, wuk_ref, wuv_ref, cos_ref, sin_ref,
                x1_ref, cg_ref, zd_ref, ckv_ref, kpe_ref, st_ref, qa_ref, qb_ref, vb_ref,
                sbuf, *, tm, hp, off, dil, taps, carry, qscale, decode):
    W = BRANCH_W

    @pl.when(pl.program_id(0) == 0)
    def _():
        sbuf[0:hp, :] = hist_ref[...]

    x1 = (x_ref[...]
          + jnp.dot(ag_ref[...], wo_ref[0:W, :], preferred_element_type=F32)
          + jnp.dot(bg_ref[...], wo_ref[W:2 * W, :], preferred_element_type=F32))
    x1_ref[...] = x1
    hb = _rms(x1, g_ref[...]).astype(BF16)

    def proj(c0, c1):
        return jnp.dot(hb, w_ref[:, c0:c1], preferred_element_type=F32)

    sbuf[hp:hp + tm, :] = proj(2 * W, 3 * W) * proj(0, W)
    gate = _silu(proj(3 * W, 4 * W)) * proj(W, 2 * W)
    for r0 in range(0, tm, CONV_ROWS):
        conv = _dw_conv_chunk(sbuf, scw_ref, r0, off, dil, taps)
        cg_ref[r0:r0 + CONV_ROWS, :] = (gate[r0:r0 + CONV_ROWS, :] * conv).astype(BF16)
    ns = st_ref.shape[0]
    st_ref[...] = sbuf[hp + tm - ns:hp + tm, :]
    if carry:
        sbuf[0:hp, :] = sbuf[tm:tm + hp, :]

    c0 = 4 * W
    cqn = _rms(proj(c0, c0 + Q_LORA), qng_ref[...]).astype(BF16)
    ckv = _rms(proj(c0 + Q_LORA, c0 + Q_LORA + KV_LORA), kvg_ref[...])
    ckv_ref[...] = ckv
    ckvb = ckv.astype(BF16)
    c1 = c0 + Q_LORA + KV_LORA
    cos = cos_ref[...]
    sin = sin_ref[...]
    kpe = _rope128(proj(c1, c1 + LANES), cos, sin, MLA_ROPE // 2)
    kpe_ref[...] = kpe
    zd_ref[...] = _silu(proj(c1 + LANES, c1 + LANES + W))

    q = jnp.dot(cqn, wuq_ref[...], preferred_element_type=F32)
    q = _rope_wide(q, cos, sin, MLA_ROPE // 2)
    if decode:
        qbf = q.astype(BF16)
        qa_ref[...] = jnp.concatenate(
            [jnp.dot(qbf[:, h * MLA_QSLOT:(h + 1) * MLA_QSLOT], wuk_ref[h],
                     preferred_element_type=F32) for h in range(MLA_H)], axis=1) * qscale
        qb_ref[...] = q * qscale
        vb_ref[...] = jnp.zeros(vb_ref.shape, vb_ref.dtype)
    else:
        qa_ref[...] = (q * qscale).astype(BF16)
        knope = jnp.dot(ckvb, wuk_ref[...], preferred_element_type=F32)
        qb_ref[...] = (knope + jnp.concatenate([kpe] * MLA_H, axis=1)).astype(BF16)
        vb_ref[...] = jnp.dot(ckvb, wuv_ref[...], preferred_element_type=F32).astype(BF16)


def _mid(x, ag, bg, hist, wo_bf, g, w_bf, scw, qng, wuq_bf, kvg, wuk_bf, wuv_bf, cos, sin,
         *, tm, dil, decode):
    rows, d = x.shape
    taps = scw.shape[0]
    hp = hist.shape[0]
    off = hp - (taps - 1) * dil
    W = BRANCH_W
    n = rows // tm
    row = lambda i: (i, 0)
    fix2 = lambda i: (0, 0)
    fix = lambda a: _resident(a.shape)
    ns = min(hp, tm)
    kern = functools.partial(_mid_kernel, tm=tm, hp=hp, off=off, dil=dil, taps=taps,
                             carry=n > 1, qscale=(MLA_NOPE + MLA_ROPE) ** -0.5 * LOG2E,
                             decode=decode)
    qw = MLA_H * MLA_QSLOT
    if decode:
        qa_shape, qa_dt, qb_dt = (rows, MLA_H * KV_LORA), F32, F32
    else:
        qa_shape, qa_dt, qb_dt = (rows, qw), BF16, BF16
    out_shape = [jax.ShapeDtypeStruct((rows, d), F32), jax.ShapeDtypeStruct((rows, W), BF16),
                 jax.ShapeDtypeStruct((rows, W), F32), jax.ShapeDtypeStruct((rows, KV_LORA), F32),
                 jax.ShapeDtypeStruct((rows, LANES), F32), jax.ShapeDtypeStruct((ns, W), F32),
                 jax.ShapeDtypeStruct(qa_shape, qa_dt), jax.ShapeDtypeStruct((rows, qw), qb_dt),
                 jax.ShapeDtypeStruct((rows, W), BF16)]
    out_specs = [pl.BlockSpec((tm, s.shape[1]), row) for s in out_shape]
    out_specs[5] = pl.BlockSpec((ns, W), fix2)
    return pl.pallas_call(
        kern,
        grid=(n,),
        in_specs=[pl.BlockSpec((tm, d), row), pl.BlockSpec((tm, W), row), pl.BlockSpec((tm, W), row),
                  fix(hist), fix(wo_bf), fix(g), fix(w_bf), fix(scw), fix(qng), fix(wuq_bf),
                  fix(kvg), fix(wuk_bf), fix(wuv_bf),
                  pl.BlockSpec((tm, LANES), row), pl.BlockSpec((tm, LANES), row)],
        out_specs=out_specs,
        out_shape=out_shape,
        scratch_shapes=[pltpu.VMEM((hp + tm, W), F32)],
        compiler_params=pltpu.CompilerParams(dimension_semantics=("arbitrary",),
                                             vmem_limit_bytes=VMEM_LIMIT),
        name="mid_decode" if decode else "mid_prompt",
    )(x, ag, bg, hist, wo_bf, g, w_bf, scw, qng, wuq_bf, kvg, wuk_bf, wuv_bf, cos, sin)


def _back_kernel(x_ref, cg_ref, dg_ref, wo_ref, g_ref, y_ref):
    W = BRANCH_W
    x2 = (x_ref[...]
          + jnp.dot(cg_ref[...], wo_ref[0:W, :], preferred_element_type=F32)
          + jnp.dot(dg_ref[...], wo_ref[W:2 * W, :], preferred_element_type=F32))
    y_ref[...] = _rms(x2, g_ref[...])


def _back(x1, cg, dg, wo_bf, g, *, tm):
    rows, d = x1.shape
    W = BRANCH_W
    row = lambda i: (i, 0)
    fix = lambda i: (0, 0)
    return pl.pallas_call(
        _back_kernel,
        grid=(rows // tm,),
        in_specs=[pl.BlockSpec((tm, d), row), pl.BlockSpec((tm, W), row), pl.BlockSpec((tm, W), row),
                  pl.BlockSpec(wo_bf.shape, fix), pl.BlockSpec((1, d), fix)],
        out_specs=pl.BlockSpec((tm, d), row),
        out_shape=jax.ShapeDtypeStruct((rows, d), F32),
        compiler_params=pltpu.CompilerParams(dimension_semantics=("arbitrary",),
                                             vmem_limit_bytes=VMEM_LIMIT),
        name="back",
    )(x1, cg, dg, wo_bf, g)


def _online_update(s, m_ref, l_ref):
    m_prev = m_ref[:, 0:1]
    m_new = jnp.maximum(m_prev, jnp.max(s, axis=-1, keepdims=True))
    alpha = jnp.exp2(m_prev - m_new)
    p = jnp.exp2(s - m_new)
    l_ref[...] = alpha * l_ref[...] + jnp.sum(p, axis=-1, keepdims=True)
    m_ref[...] = jnp.broadcast_to(m_new, m_ref.shape)
    return alpha, p


def _new_token_mask(s, t_new):
    tok = lax.rem(lax.broadcasted_iota(jnp.int32, s.shape, 0), t_new)
    key = lax.broadcasted_iota(jnp.int32, s.shape, 1)
    return jnp.where(key <= tok, s, -jnp.inf)


def _dec_diff_kernel(pt_ref, q_ref, kn_ref, vn_ref, *refs, pages, t_new):
    k_refs = refs[:pages]
    v_refs = refs[pages:2 * pages]
    o_ref, m_ref, l_ref, acc_ref = refs[2 * pages:]
    j = pl.program_id(1)

    @pl.when(j == 0)
    def _():
        m_ref[...] = jnp.full(m_ref.shape, NEG_INIT, F32)
        l_ref[...] = jnp.zeros(l_ref.shape, F32)
        acc_ref[...] = jnp.zeros(acc_ref.shape, F32)

    def head(h, keys, vals, mask):
        lanes = slice(h * LANES, (h + 1) * LANES)
        qh = q_ref[h]
        s = jnp.concatenate(
            [lax.dot_general(qh, kr[:, lanes].astype(BF16), _NT, preferred_element_type=F32)
             for kr in keys], axis=1)
        if mask:
            s = _new_token_mask(s, t_new)
        alpha, p = _online_update(s, m_ref.at[h], l_ref.at[h])
        pb = p.astype(BF16)
        nk = keys[0].shape[0]
        pv = None
        for i, vr in enumerate(vals):
            t = jnp.dot(pb[:, i * nk:(i + 1) * nk], vr[:, lanes].astype(BF16),
                        preferred_element_type=F32)
            pv = t if pv is None else pv + t
        acc_ref[h] = alpha * acc_ref[h] + pv

    for h in range(DIFF_H):
        head(h, k_refs, v_refs, False)

    @pl.when(j == pl.num_programs(1) - 1)
    def _():
        for h in range(DIFF_H):
            head(h, [kn_ref], [vn_ref], True)
            o_ref[h] = acc_ref[h] / l_ref[h][:, 0:1]


def _dec_diff(page_table, qd, kn, vn, cache_k, cache_v, *, pages, t_new):
    bd, n_pages = page_table.shape
    rows = qd.shape[2]
    width = cache_k.shape[-1]
    nn = kn.shape[1]

    def page_spec(i):
        return pl.BlockSpec((None, PAGE, width), lambda b, j, pt: (pt[b, j * pages + i], 0, 0))

    per_b = lambda b, j, pt: (b, 0, 0, 0)
    per_b3 = lambda b, j, pt: (b, 0, 0)
    grid_spec = pltpu.PrefetchScalarGridSpec(
        num_scalar_prefetch=1,
        grid=(bd, n_pages // pages),
        in_specs=[pl.BlockSpec((None, DIFF_H, rows, LANES), per_b),
                  pl.BlockSpec((None, nn, width), per_b3), pl.BlockSpec((None, nn, width), per_b3)]
                 + [page_spec(i) for i in range(pages)] * 2,
        out_specs=pl.BlockSpec((None, DIFF_H, rows, LANES), per_b),
        scratch_shapes=[pltpu.VMEM((DIFF_H, rows, LANES), F32)] * 3,
    )
    return pl.pallas_call(
        functools.partial(_dec_diff_kernel, pages=pages, t_new=t_new),
        grid_spec=grid_spec,
        out_shape=jax.ShapeDtypeStruct((bd, DIFF_H, rows, LANES), F32),
        compiler_params=pltpu.CompilerParams(dimension_semantics=("arbitrary", "arbitrary"),
                                             vmem_limit_bytes=VMEM_LIMIT),
        name="decode_diff",
    )(page_table, qd, kn, vn, *([cache_k] * pages), *([cache_v] * pages))


def _dec_mla_kernel(pt_ref, qa_ref, qp_ref, cn_ref, en_ref, *refs, pages, t_new):
    c_refs = refs[:pages]
    e_refs = refs[pages:2 * pages]
    o_ref, m_ref, l_ref, acc_ref = refs[2 * pages:]
    j = pl.program_id(1)

    @pl.when(j == 0)
    def _():
        m_ref[...] = jnp.full(m_ref.shape, NEG_INIT, F32)
        l_ref[...] = jnp.zeros(l_ref.shape, F32)
        acc_ref[...] = jnp.zeros(acc_ref.shape, F32)

    def attend(lat_refs, rope_refs, mask):
        qa = qa_ref[...]
        qp = qp_ref[...]
        lat = [r[...].astype(BF16) for r in lat_refs]
        s = jnp.concatenate(
            [lax.dot_general(qa, c, _NT, preferred_element_type=F32)
             + lax.dot_general(qp, e[...].astype(BF16), _NT, preferred_element_type=F32)
             for c, e in zip(lat, rope_refs)], axis=1)
        if mask:
            s = _new_token_mask(s, t_new)
        alpha, p = _online_update(s, m_ref, l_ref)
        pb = p.astype(BF16)
        nk = lat[0].shape[0]
        pv = None
        for i, c in enumerate(lat):
            t = jnp.dot(pb[:, i * nk:(i + 1) * nk], c, preferred_element_type=F32)
            pv = t if pv is None else pv + t
        acc_ref[...] = alpha * acc_ref[...] + pv

    attend(c_refs, e_refs, False)

    @pl.when(j == pl.num_programs(1) - 1)
    def _():
        attend([cn_ref], [en_ref], True)
        o_ref[...] = acc_ref[...] / l_ref[:, 0:1]


def _dec_mla(page_table, qa, qp, cn, en, cache_c, cache_e, *, pages, t_new):
    bd, n_pages = page_table.shape
    rows = qa.shape[1]
    nn = cn.shape[1]
    per_b = lambda b, j, pt: (b, 0, 0)

    def page_spec(w, i):
        return pl.BlockSpec((None, PAGE, w), lambda b, j, pt: (pt[b, j * pages + i], 0, 0))

    grid_spec = pltpu.PrefetchScalarGridSpec(
        num_scalar_prefetch=1,
        grid=(bd, n_pages // pages),
        in_specs=[pl.BlockSpec((None, rows, KV_LORA), per_b), pl.BlockSpec((None, rows, MLA_ROPE), per_b),
                  pl.BlockSpec((None, nn, KV_LORA), per_b), pl.BlockSpec((None, nn, MLA_ROPE), per_b)]
                 + [page_spec(KV_LORA, i) for i in range(pages)]
                 + [page_spec(MLA_ROPE, i) for i in range(pages)],
        out_specs=pl.BlockSpec((None, rows, KV_LORA), per_b),
        scratch_shapes=[pltpu.VMEM((rows, LANES), F32), pltpu.VMEM((rows, LANES), F32),
                        pltpu.VMEM((rows, KV_LORA), F32)],
    )
    return pl.pallas_call(
        functools.partial(_dec_mla_kernel, pages=pages, t_new=t_new),
        grid_spec=grid_spec,
        out_shape=jax.ShapeDtypeStruct((bd, rows, KV_LORA), F32),
        compiler_params=pltpu.CompilerParams(dimension_semantics=("arbitrary", "arbitrary"),
                                             vmem_limit_bytes=VMEM_LIMIT),
        name="decode_mla",
    )(page_table, qa, qp, cn, en, *([cache_c] * pages), *([cache_e] * pages))


def _diff_post_kernel(o0_ref, o1_ref, z_ref, lq1_ref, lk1_ref, lq2_ref, lk2_ref, sg_ref, bg_ref,
                      *, lam_init):
    lam = _lam(lq1_ref, lk1_ref, lq2_ref, lk2_ref, lam_init)
    o = o0_ref[...] - lam * o1_ref[...]
    outs = []
    for h in range(DIFF_H):
        outs.append(_rms(o[:, h * DIFF_DV:(h + 1) * DIFF_DV], sg_ref[...]) * (1.0 - lam_init))
    bg_ref[...] = (z_ref[...] * jnp.concatenate(outs, axis=1)).astype(BF16)


def _diff_post(o0, o1, z, lam_params, sg, lam_init):
    return pl.pallas_call(
        functools.partial(_diff_post_kernel, lam_init=lam_init),
        out_shape=jax.ShapeDtypeStruct(o0.shape, BF16),
        name="diff_post",
    )(o0, o1, z, *lam_params, sg)


def _mla_post_kernel(o_ref, z_ref, wuv_ref, dg_ref):
    d = jnp.concatenate(
        [jnp.dot(o_ref[h].astype(BF16), wuv_ref[h], preferred_element_type=F32)
         for h in range(MLA_H)], axis=1)
    dg_ref[...] = (z_ref[...] * d).astype(BF16)


def _mla_post(o, z, wuv_heads_bf):
    return pl.pallas_call(
        _mla_post_kernel,
        out_shape=jax.ShapeDtypeStruct(z.shape, BF16),
        name="mla_post",
    )(o, z, wuv_heads_bf)


def _rope_tables(pos, d, lane0):
    inv = ROPE_THETA ** (-jnp.arange(0, d, 2, dtype=F32) / d)
    ang = pos.astype(F32)[:, None] * inv[None, :]
    cos = jnp.cos(ang)
    sin = jnp.sin(ang)
    cos_d = jnp.concatenate([cos, cos], axis=1)
    sin_d = jnp.concatenate([-sin, sin], axis=1)
    r = pos.shape[0]
    if lane0 == 0 and LANES % d == 0:
        reps = LANES // d
        return jnp.tile(cos_d, (1, reps)), jnp.tile(sin_d, (1, reps))
    pad = LANES - lane0 - d
    cos_t = jnp.concatenate([jnp.ones((r, lane0), F32), cos_d, jnp.ones((r, pad), F32)], axis=1)
    sin_t = jnp.concatenate([jnp.zeros((r, lane0), F32), sin_d, jnp.zeros((r, pad), F32)], axis=1)
    return cos_t, sin_t


def _row_tile(rows, want):
    tm = min(rows, want)
    assert rows % tm == 0 and tm % CONV_ROWS == 0
    return tm


def kernel(x_prompt, x_sample, cache_diff_k, cache_diff_v, state_conf, cache_mla_ckv,
           cache_mla_krope, state_sconv, page_table, norm_g, final_norm_g, w_in_even,
           conf_dw_w, conf_dw_b, conf_ln_g, conf_ln_b, lam_q1, lam_k1, lam_q2, lam_k2,
           diff_subln_g, w_out_even, w_in_odd, sconv_w, mla_q_norm_g, w_uq, mla_kv_norm_g,
           w_uk, w_uv, w_out_odd):
    assert norm_g.shape[0] == 2, "one even and one odd layer"
    B, S, D = x_prompt.shape
    Bd, T, _ = x_sample.shape
    assert B == 1
    n_pages = page_table.shape[1]
    past_len = n_pages * PAGE
    W = BRANCH_W
    conf_k = conf_dw_w.shape[1]
    sconv_k = sconv_w.shape[1]
    lam_init = 0.8 - 0.6 * math.exp(-0.3 * 0)
    row2 = lambda a: a.reshape(1, -1)

    w_even_bf = w_in_even[0].astype(BF16)
    wo_even_bf = w_out_even[0].astype(BF16)
    wo_odd_bf = w_out_odd[0].astype(BF16)
    wi = w_in_odd[0]
    c_kr = 4 * W + Q_LORA + KV_LORA
    kr_slot = jnp.zeros((D, LANES), F32).at[:, ROPE_LANE0:ROPE_LANE0 + MLA_ROPE].set(
        wi[:, c_kr:c_kr + MLA_ROPE])
    w_odd_bf = jnp.concatenate([wi[:, :c_kr], kr_slot, wi[:, c_kr + MLA_ROPE:]], axis=1).astype(BF16)
    qd = MLA_NOPE + MLA_ROPE
    wuq_bf = jnp.pad(w_uq[0].reshape(Q_LORA, MLA_H, qd),
                     ((0, 0), (0, 0), (0, MLA_QSLOT - qd))).reshape(Q_LORA, -1).astype(BF16)
    wuk_slot_bf = jnp.pad(w_uk[0], ((0, 0), (0, 0), (0, MLA_QSLOT - MLA_NOPE))).reshape(
        KV_LORA, -1).astype(BF16)
    wuk_t_bf = jnp.pad(jnp.transpose(w_uk[0], (1, 2, 0)),
                       ((0, 0), (0, MLA_QSLOT - MLA_NOPE), (0, 0))).astype(BF16)
    wuv_bf = w_uv[0].reshape(KV_LORA, -1).astype(BF16)
    wuv_heads_bf = jnp.transpose(w_uv[0], (1, 0, 2)).astype(BF16)
    lam_params = (row2(lam_q1[0]), row2(lam_k1[0]), row2(lam_q2[0]), row2(lam_k2[0]))
    sg = row2(diff_subln_g[0])

    pos_p = jnp.arange(S)
    cos_d, sin_d = _rope_tables(pos_p, DIFF_DK, 0)
    cos_m, sin_m = _rope_tables(pos_p, MLA_ROPE, ROPE_LANE0)
    tm = _row_tile(S, 256)
    tq = min(S, 256)
    tk = min(S, 512)
    xp = x_prompt[0]
    hist_c = jnp.zeros((32, W), F32)
    ag, zb, qb, kf, kb, vf, vb, st_c = _even_front(
        xp, hist_c, row2(norm_g[0]), w_even_bf, cos_d, sin_d, conf_dw_w[0], row2(conf_dw_b[0]),
        row2(conf_ln_g[0]), row2(conf_ln_b[0]), tm=tm, dil=1)
    bg = _flash(functools.partial(_flash_diff_kernel, lam_init=lam_init), qb, kb, vb, zb,
                (*lam_params, sg), dk=LANES, tq=tq, tk=tk, name="flash_diff")
    hist_s = jnp.zeros((8, W), F32)
    x1, cg, zd, ckv_p, kpe_p, st_s, qm, km, vm = _mid(
        xp, ag, bg, hist_s, wo_even_bf, row2(norm_g[1]), w_odd_bf, sconv_w[0],
        row2(mla_q_norm_g[0]), wuq_bf, row2(mla_kv_norm_g[0]), wuk_slot_bf, wuv_bf, cos_m, sin_m,
        tm=tm, dil=1, decode=False)
    dg = _flash(_flash_mla_kernel, qm, km, vm, zd, (), dk=2 * MLA_QSLOT, tq=tq, tk=tk,
                name="flash_mla")
    y_p = _back(x1, cg, dg, wo_odd_bf, row2(final_norm_g), tm=tm)

    y_prompt = y_p[None]
    diff_k_prompt = kf.reshape(1, 1, S, DIFF_H, 2, DIFF_DK)
    diff_v_prompt = vf.reshape(1, 1, S, DIFF_H, DIFF_DV)
    conf_state_prompt = st_c[32 - (conf_k - 1):][None, None]
    mla_ckv_prompt = ckv_p[None, None]
    mla_krope_prompt = kpe_p[:, ROPE_LANE0:ROPE_LANE0 + MLA_ROPE][None, None]
    sconv_state_prompt = st_s[8 - (sconv_k - 1):][None, None]

    R = T * Bd
    tmajor = lambda a: jnp.swapaxes(a, 0, 1).reshape((-1,) + a.shape[2:])
    bmajor = lambda a: jnp.swapaxes(a.reshape((-1, Bd) + a.shape[1:]), 0, 1)
    pos_s = jnp.repeat(past_len + jnp.arange(T), Bd)
    cos_d, sin_d = _rope_tables(pos_s, DIFF_DK, 0)
    cos_m, sin_m = _rope_tables(pos_s, MLA_ROPE, ROPE_LANE0)
    xs = tmajor(x_sample)
    ag, zb, qb, kf, kb, vf, vb, st_c = _even_front(
        xs, tmajor(state_conf[0]), row2(norm_g[0]), w_even_bf, cos_d, sin_d, conf_dw_w[0],
        row2(conf_dw_b[0]), row2(conf_ln_g[0]), row2(conf_ln_b[0]), tm=R, dil=Bd)

    rows_d = 16
    q4 = jnp.transpose(qb.reshape(T, Bd, DIFF_H, LANES), (1, 2, 0, 3))
    lane = jnp.arange(LANES)
    q_maps = jnp.concatenate([jnp.where(lane < DIFF_DK, q4, 0), jnp.where(lane >= DIFF_DK, q4, 0)],
                             axis=2)
    q_dec = jnp.pad(q_maps, ((0, 0), (0, 0), (0, rows_d - 2 * T), (0, 0)))
    new_rows = 16
    pad_new = lambda a: jnp.pad(bmajor(a), ((0, 0), (0, new_rows - T), (0, 0)))
    o_d = _dec_diff(page_table, q_dec, pad_new(kf), pad_new(vf),
                    cache_diff_k[0].reshape(-1, PAGE, W), cache_diff_v[0].reshape(-1, PAGE, W),
                    pages=8, t_new=T)
    o_maps = jnp.transpose(o_d[:, :, :2 * T].reshape(Bd, DIFF_H, 2, T, DIFF_DV), (2, 3, 0, 1, 4))
    o_maps = o_maps.reshape(2, R, W)
    bg = _diff_post(o_maps[0], o_maps[1], zb, lam_params, sg, lam_init)

    x1, cg, zd, ckv_s, kpe_s, st_s, qa, qr, _ = _mid(
        xs, ag, bg, tmajor(state_sconv[0]), wo_even_bf, row2(norm_g[1]), w_odd_bf, sconv_w[0],
        row2(mla_q_norm_g[0]), wuq_bf, row2(mla_kv_norm_g[0]), wuk_t_bf, wuv_bf, cos_m, sin_m,
        tm=R, dil=Bd, decode=True)
    qa_dec = jnp.transpose(qa.reshape(T, Bd, MLA_H, KV_LORA), (1, 2, 0, 3)).reshape(
        Bd, MLA_H * T, KV_LORA).astype(BF16)
    qp_dec = jnp.transpose(
        qr.reshape(T, Bd, MLA_H, MLA_QSLOT)[..., ROPE_LANE0:ROPE_LANE0 + MLA_ROPE],
        (1, 2, 0, 3)).reshape(Bd, MLA_H * T, MLA_ROPE).astype(BF16)
    kpe_new = kpe_s[:, ROPE_LANE0:ROPE_LANE0 + MLA_ROPE]
    o_m = _dec_mla(page_table, qa_dec, qp_dec, pad_new(ckv_s), pad_new(kpe_new),
                   cache_mla_ckv[0], cache_mla_krope[0], pages=16, t_new=T)
    o_heads = jnp.transpose(o_m.reshape(Bd, MLA_H, T, KV_LORA), (1, 2, 0, 3)).reshape(
        MLA_H, R, KV_LORA)
    dg = _mla_post(o_heads, zd, wuv_heads_bf)
    y_s = _back(x1, cg, dg, wo_odd_bf, row2(final_norm_g), tm=R)

    y_sample = bmajor(y_s)
    diff_k_sample = bmajor(kf).reshape(1, Bd, T, DIFF_H, 2, DIFF_DK)
    diff_v_sample = bmajor(vf).reshape(1, Bd, T, DIFF_H, DIFF_DV)
    keep = lambda prev, new, k: jnp.concatenate([prev[:, new.shape[1]:], new], axis=1)[:, -(k - 1):]
    conf_state_sample = keep(state_conf[0], bmajor(st_c), conf_k)[None]
    mla_ckv_sample = bmajor(ckv_s)[None]
    mla_krope_sample = bmajor(kpe_new)[None]
    sconv_state_sample = keep(state_sconv[0], bmajor(st_s), sconv_k)[None]

    return (y_prompt, y_sample,
            diff_k_prompt, diff_v_prompt, conf_state_prompt,
            mla_ckv_prompt, mla_krope_prompt, sconv_state_prompt,
            diff_k_sample, diff_v_sample, conf_state_sample,
            mla_ckv_sample, mla_krope_sample, sconv_state_sample)
```

```python
import functools
import math

import jax
import jax.numpy as jnp
from jax import lax
from jax.experimental import pallas as pl
from jax.experimental.pallas import tpu as pltpu

F32 = jnp.float32
BF16 = jnp.bfloat16

NORM_EPS = 1e-6
ROPE_THETA = 10000.0
LOG2E = 1.4426950408889634
NEG_INIT = -1e30

LANES = 128
BRANCH_W = 512
DIFF_H, DIFF_DK, DIFF_DV = 4, 64, 128
MLA_H, MLA_NOPE, MLA_ROPE, MLA_DV = 8, 64, 32, 64
Q_LORA = KV_LORA = 256
MLA_QSLOT = 128
ROPE_LANE0 = MLA_NOPE
PAGE = 128

CONV_ROWS = 32
VMEM_LIMIT = 56 * 1024 * 1024

_NT = (((1,), (1,)), ((), ()))


def _rms(x, g):
    return x * lax.rsqrt(jnp.mean(x * x, axis=-1, keepdims=True) + NORM_EPS) * g


def _silu(x):
    return x * jax.nn.sigmoid(x)


def _rope128(x, cos, sin, half):
    lane = lax.broadcasted_iota(jnp.int32, x.shape, 1)
    first = (lane & (2 * half - 1)) < half
    up = pltpu.roll(x, LANES - half, 1)
    dn = pltpu.roll(x, half, 1)
    return x * cos + jnp.where(first, up, dn) * sin


def _rope_wide(x, cos, sin, half):
    n = x.shape[1] // LANES
    return jnp.concatenate(
        [_rope128(x[:, c * LANES:(c + 1) * LANES], cos, sin, half) for c in range(n)], axis=1)


def _resident(shape):
    return pl.BlockSpec(shape, lambda *_: (0,) * len(shape), pipeline_mode=pl.Buffered(1))


def _dw_conv_chunk(buf, w_ref, r0, off, dil, taps):
    acc = None
    for j in range(taps):
        a = r0 + off + j * dil
        t = buf[a:a + CONV_ROWS, :] * w_ref[j:j + 1, :]
        acc = t if acc is None else acc + t
    return acc


def _even_front_kernel(x_ref, hist_ref, g_ref, w_ref, cos_ref, sin_ref, dww_ref, dwb_ref,
                       lng_ref, lnb_ref,
                       ag_ref, zb_ref, qb_ref, kf_ref, kb_ref, vf_ref, vb_ref, st_ref,
                       ubuf, *, tm, hp, off, dil, taps, carry, qscale):
    W = BRANCH_W

    @pl.when(pl.program_id(0) == 0)
    def _():
        ubuf[0:hp, :] = hist_ref[...]

    hb = _rms(x_ref[...], g_ref[...]).astype(BF16)

    def proj(c):
        return jnp.dot(hb, w_ref[:, c * W:(c + 1) * W], preferred_element_type=F32)

    ubuf[hp:hp + tm, :] = proj(0) * jax.nn.sigmoid(proj(1))
    za = proj(2)
    for r0 in range(0, tm, CONV_ROWS):
        c = _dw_conv_chunk(ubuf, dww_ref, r0, off, dil, taps) + dwb_ref[...]
        d = c - jnp.mean(c, axis=-1, keepdims=True)
        y = d * lax.rsqrt(jnp.mean(d * d, axis=-1, keepdims=True) + NORM_EPS)
        y = y * lng_ref[...] + lnb_ref[...]
        ag_ref[r0:r0 + CONV_ROWS, :] = (_silu(za[r0:r0 + CONV_ROWS, :]) * _silu(y)).astype(BF16)
    ns = st_ref.shape[0]
    st_ref[...] = ubuf[hp + tm - ns:hp + tm, :]
    if carry:
        ubuf[0:hp, :] = ubuf[tm:tm + hp, :]

    cos = cos_ref[...]
    sin = sin_ref[...]
    q = _rope_wide(proj(3), cos, sin, DIFF_DK // 2)
    qb_ref[...] = (q * qscale).astype(BF16)
    k = _rope_wide(proj(4), cos, sin, DIFF_DK // 2)
    kf_ref[...] = k
    kb_ref[...] = k.astype(BF16)
    v = proj(5)
    vf_ref[...] = v
    vb_ref[...] = v.astype(BF16)
    zb_ref[...] = _silu(proj(6))


def _even_front(x, hist, g, w_bf, cos, sin, dww, dwb, lng, lnb, *, tm, dil):
    rows, d = x.shape
    taps = dww.shape[0]
    hp = hist.shape[0]
    off = hp - (taps - 1) * dil
    W = BRANCH_W
    n = rows // tm
    row = lambda i: (i, 0)
    fix = lambda i: (0, 0)
    kern = functools.partial(_even_front_kernel, tm=tm, hp=hp, off=off, dil=dil, taps=taps,
                             carry=n > 1, qscale=DIFF_DK ** -0.5 * LOG2E)
    wide = lambda dt: jax.ShapeDtypeStruct((rows, W), dt)
    ns = min(hp, tm)
    return pl.pallas_call(
        kern,
        grid=(n,),
        in_specs=[pl.BlockSpec((tm, d), row), _resident((hp, W)),
                  pl.BlockSpec((1, d), fix), _resident(w_bf.shape),
                  pl.BlockSpec((tm, LANES), row), pl.BlockSpec((tm, LANES), row),
                  pl.BlockSpec((taps, W), fix), pl.BlockSpec((1, W), fix),
                  pl.BlockSpec((1, W), fix), pl.BlockSpec((1, W), fix)],
        out_specs=[pl.BlockSpec((tm, W), row)] * 7 + [pl.BlockSpec((ns, W), fix)],
        out_shape=[wide(BF16), wide(F32), wide(BF16), wide(F32), wide(BF16), wide(F32),
                   wide(BF16), jax.ShapeDtypeStruct((ns, W), F32)],
        scratch_shapes=[pltpu.VMEM((hp + tm, W), F32)],
        compiler_params=pltpu.CompilerParams(dimension_semantics=("arbitrary",),
                                             vmem_limit_bytes=VMEM_LIMIT),
        name="even_front",
    )(x, hist, g, w_bf, cos, sin, dww, dwb, lng, lnb)


def _flash_body(qt_ref, k_ref, vt_ref, q2_ref, s0_ref, s1_ref, m_ref, l_ref, acc_ref, *, tq, tk):
    qi = pl.program_id(1)
    qt = qt_ref[...]
    dk = qt.shape[0]
    feat = lax.broadcasted_iota(jnp.int32, qt.shape, 0)
    zero = jnp.zeros_like(qt)
    q2_ref[...] = jnp.concatenate([jnp.where(feat < dk // 2, qt, zero),
                                   jnp.where(feat >= dk // 2, qt, zero)], axis=1)
    m_ref[...] = jnp.full(m_ref.shape, NEG_INIT, F32)
    l_ref[...] = jnp.zeros(l_ref.shape, F32)
    acc_ref[...] = jnp.zeros(acc_ref.shape, F32)

    def scores(kb, s_ref):
        ks = pl.multiple_of(kb * tk, tk)
        s_ref[...] = jnp.dot(k_ref[pl.ds(ks, tk), :], q2_ref[...],
                             preferred_element_type=F32)

    def accumulate(kb, s_ref, masked):
        st = s_ref[...]
        if masked:
            key = kb * tk + lax.broadcasted_iota(jnp.int32, st.shape, 0)
            qpos = qi * tq + lax.rem(lax.broadcasted_iota(jnp.int32, st.shape, 1), tq)
            st = jnp.where(key <= qpos, st, -jnp.inf)
        m_prev = m_ref[...]
        m_new = jnp.maximum(m_prev, jnp.max(st, axis=0, keepdims=True))
        alpha = jnp.exp2(m_prev - m_new)
        p = jnp.exp2(st - m_new)
        l_ref[...] = alpha * l_ref[...] + jnp.sum(p, axis=0, keepdims=True)
        acc_ref[...] = alpha * acc_ref[...] + jnp.dot(
            vt_ref[kb], p.astype(BF16), preferred_element_type=F32)
        m_ref[...] = m_new

    nfull = (qi * tq) // tk
    scores(0, s0_ref)

    def pair(i, c):
        scores(2 * i + 1, s1_ref)
        accumulate(2 * i, s0_ref, False)
        scores(2 * i + 2, s0_ref)
        accumulate(2 * i + 1, s1_ref, False)
        return c

    lax.fori_loop(0, nfull // 2, pair, 0)
    odd = lax.rem(nfull, 2) == 1

    @pl.when(odd)
    def _():
        scores(nfull, s1_ref)
        accumulate(nfull - 1, s0_ref, False)
        accumulate(nfull, s1_ref, True)

    @pl.when(jnp.logical_not(odd))
    def _():
        accumulate(nfull, s0_ref, True)

    ot = acc_ref[...] / l_ref[...]
    return ot[:, :tq], ot[:, tq:]


def _lam(lq1_ref, lk1_ref, lq2_ref, lk2_ref, lam_init):
    a = jnp.sum(lq1_ref[...] * lk1_ref[...], axis=-1, keepdims=True)
    b = jnp.sum(lq2_ref[...] * lk2_ref[...], axis=-1, keepdims=True)
    return jnp.exp(a) - jnp.exp(b) + lam_init


def _flash_diff_kernel(qt_ref, k_ref, vt_ref, z_ref, lq1_ref, lk1_ref, lq2_ref, lk2_ref, sgt_ref,
                       o_ref, *scratch, tq, tk, lam_init):
    o0, o1 = _flash_body(qt_ref, k_ref, vt_ref, *scratch, tq=tq, tk=tk)
    ot = o0 - _lam(lq1_ref, lk1_ref, lq2_ref, lk2_ref, lam_init) * o1
    ot = ot * lax.rsqrt(jnp.mean(ot * ot, axis=0, keepdims=True) + NORM_EPS)
    ot = ot * sgt_ref[...] * (1.0 - lam_init)
    o_ref[...] = (z_ref[...] * ot.T).astype(BF16)


def _flash_mla_kernel(qt_ref, k_ref, vt_ref, z_ref, o_ref, *scratch, tq, tk):
    o0, o1 = _flash_body(qt_ref, k_ref, vt_ref, *scratch, tq=tq, tk=tk)
    feat = lax.broadcasted_iota(jnp.int32, o0.shape, 0)
    o_ref[...] = (z_ref[...] * jnp.where(feat < MLA_DV, o0, o1).T).astype(BF16)


def _flash(kern, q, k, v, z, extra, *, dk, tq, tk, name):
    S = q.shape[0]
    groups = q.shape[1] // dk
    nk = S // tk
    qt = q.T
    vt = jnp.transpose(v.reshape(nk, tk, groups, LANES), (2, 0, 3, 1))
    fix = lambda g, i: (0, 0)
    return pl.pallas_call(
        functools.partial(kern, tq=tq, tk=tk),
        grid=(groups, S // tq),
        in_specs=[pl.BlockSpec((dk, tq), lambda g, i: (g, i)),
                  pl.BlockSpec((S, dk), lambda g, i: (0, g)),
                  pl.BlockSpec((None, nk, LANES, tk), lambda g, i: (g, 0, 0, 0)),
                  pl.BlockSpec((tq, LANES), lambda g, i: (i, g))]
                 + [pl.BlockSpec(e.shape, fix) for e in extra],
        out_specs=pl.BlockSpec((tq, LANES), lambda g, i: (i, g)),
        out_shape=jax.ShapeDtypeStruct((S, groups * LANES), BF16),
        scratch_shapes=[pltpu.VMEM((dk, 2 * tq), BF16),
                        pltpu.VMEM((tk, 2 * tq), F32), pltpu.VMEM((tk, 2 * tq), F32),
                        pltpu.VMEM((1, 2 * tq), F32), pltpu.VMEM((1, 2 * tq), F32),
                        pltpu.VMEM((LANES, 2 * tq), F32)],
        compiler_params=pltpu.CompilerParams(dimension_semantics=("arbitrary", "arbitrary"),
                                             vmem_limit_bytes=VMEM_LIMIT),
        name=name,
    )(qt, k, vt, z, *extra)


def _mid_kernel(x_ref, ag_ref, bg_ref, hist_ref, wo_ref, g_ref, w_ref, scw_ref, qng_ref, wuq_ref,
                kvg_ref, wuk_ref, wuv_ref, cos_ref, sin_ref,
                x1_ref, cg_ref, zd_ref, ckv_ref, kpe_ref, st_ref, qa_ref, qb_ref, vb_ref,
                sbuf, *, tm, hp, off, dil, taps, carry, qscale, decode):
    W = BRANCH_W

    @pl.when(pl.program_id(0) == 0)
    def _():
        sbuf[0:hp, :] = hist_ref[...]

    x1 = (x_ref[...]
          + jnp.dot(ag_ref[...], wo_ref[0:W, :], preferred_element_type=F32)
          + jnp.dot(bg_ref[...], wo_ref[W:2 * W, :], preferred_element_type=F32))
    x1_ref[...] = x1
    hb = _rms(x1, g_ref[...]).astype(BF16)

    def proj(c0, c1):
        return jnp.dot(hb, w_ref[:, c0:c1], preferred_element_type=F32)

    sbuf[hp:hp + tm, :] = proj(2 * W, 3 * W) * proj(0, W)
    gate = _silu(proj(3 * W, 4 * W)) * proj(W, 2 * W)
    for r0 in range(0, tm, CONV_ROWS):
        conv = _dw_conv_chunk(sbuf, scw_ref, r0, off, dil, taps)
        cg_ref[r0:r0 + CONV_ROWS, :] = (gate[r0:r0 + CONV_ROWS, :] * conv).astype(BF16)
    ns = st_ref.shape[0]
    st_ref[...] = sbuf[hp + tm - ns:hp + tm, :]
    if carry:
        sbuf[0:hp, :] = sbuf[tm:tm + hp, :]

    c0 = 4 * W
    cqn = _rms(proj(c0, c0 + Q_LORA), qng_ref[...]).astype(BF16)
    ckv = _rms(proj(c0 + Q_LORA, c0 + Q_LORA + KV_LORA), kvg_ref[...])
    ckv_ref[...] = ckv
    ckvb = ckv.astype(BF16)
    c1 = c0 + Q_LORA + KV_LORA
    cos = cos_ref[...]
    sin = sin_ref[...]
    kpe = _rope128(proj(c1, c1 + LANES), cos, sin, MLA_ROPE // 2)
    kpe_ref[...] = kpe
    zd_ref[...] = _silu(proj(c1 + LANES, c1 + LANES + W))

    q = jnp.dot(cqn, wuq_ref[...], preferred_element_type=F32)
    q = _rope_wide(q, cos, sin, MLA_ROPE // 2)
    if decode:
        qbf = q.astype(BF16)
        qa_ref[...] = jnp.concatenate(
            [jnp.dot(qbf[:, h * MLA_QSLOT:(h + 1) * MLA_QSLOT], wuk_ref[h],
                     preferred_element_type=F32) for h in range(MLA_H)], axis=1) * qscale
        qb_ref[...] = q * qscale
        vb_ref[...] = jnp.zeros(vb_ref.shape, vb_ref.dtype)
    else:
        qa_ref[...] = (q * qscale).astype(BF16)
        knope = jnp.dot(ckvb, wuk_ref[...], preferred_element_type=F32)
        qb_ref[...] = (knope + jnp.concatenate([kpe] * MLA_H, axis=1)).astype(BF16)
        vb_ref[...] = jnp.dot(ckvb, wuv_ref[...], preferred_element_type=F32).astype(BF16)


def _mid(x, ag, bg, hist, wo_bf, g, w_bf, scw, qng, wuq_bf, kvg, wuk_bf, wuv_bf, cos, sin,
         *, tm, dil, decode):
    rows, d = x.shape
    taps = scw.shape[0]
    hp = hist.shape[0]
    off = hp - (taps - 1) * dil
    W = BRANCH_W
    n = rows // tm
    row = lambda i: (i, 0)
    fix2 = lambda i: (0, 0)
    fix = lambda a: _resident(a.shape)
    ns = min(hp, tm)
    kern = functools.partial(_mid_kernel, tm=tm, hp=hp, off=off, dil=dil, taps=taps,
                             carry=n > 1, qscale=(MLA_NOPE + MLA_ROPE) ** -0.5 * LOG2E,
                             decode=decode)
    qw = MLA_H * MLA_QSLOT
    if decode:
        qa_shape, qa_dt, qb_dt = (rows, MLA_H * KV_LORA), F32, F32
    else:
        qa_shape, qa_dt, qb_dt = (rows, qw), BF16, BF16
    out_shape = [jax.ShapeDtypeStruct((rows, d), F32), jax.ShapeDtypeStruct((rows, W), BF16),
                 jax.ShapeDtypeStruct((rows, W), F32), jax.ShapeDtypeStruct((rows, KV_LORA), F32),
                 jax.ShapeDtypeStruct((rows, LANES), F32), jax.ShapeDtypeStruct((ns, W), F32),
                 jax.ShapeDtypeStruct(qa_shape, qa_dt), jax.ShapeDtypeStruct((rows, qw), qb_dt),
                 jax.ShapeDtypeStruct((rows, W), BF16)]
    out_specs = [pl.BlockSpec((tm, s.shape[1]), row) for s in out_shape]
    out_specs[5] = pl.BlockSpec((ns, W), fix2)
    return pl.pallas_call(
        kern,
        grid=(n,),
        in_specs=[pl.BlockSpec((tm, d), row), pl.BlockSpec((tm, W), row), pl.BlockSpec((tm, W), row),
                  fix(hist), fix(wo_bf), fix(g), fix(w_bf), fix(scw), fix(qng), fix(wuq_bf),
                  fix(kvg), fix(wuk_bf), fix(wuv_bf),
                  pl.BlockSpec((tm, LANES), row), pl.BlockSpec((tm, LANES), row)],
        out_specs=out_specs,
        out_shape=out_shape,
        scratch_shapes=[pltpu.VMEM((hp + tm, W), F32)],
        compiler_params=pltpu.CompilerParams(dimension_semantics=("arbitrary",),
                                             vmem_limit_bytes=VMEM_LIMIT),
        name="mid_decode" if decode else "mid_prompt",
    )(x, ag, bg, hist, wo_bf, g, w_bf, scw, qng, wuq_bf, kvg, wuk_bf, wuv_bf, cos, sin)


def _back_kernel(x_ref, cg_ref, dg_ref, wo_ref, g_ref, y_ref):
    W = BRANCH_W
    x2 = (x_ref[...]
          + jnp.dot(cg_ref[...], wo_ref[0:W, :], preferred_element_type=F32)
          + jnp.dot(dg_ref[...], wo_ref[W:2 * W, :], preferred_element_type=F32))
    y_ref[...] = _rms(x2, g_ref[...])


def _back(x1, cg, dg, wo_bf, g, *, tm):
    rows, d = x1.shape
    W = BRANCH_W
    row = lambda i: (i, 0)
    fix = lambda i: (0, 0)
    return pl.pallas_call(
        _back_kernel,
        grid=(rows // tm,),
        in_specs=[pl.BlockSpec((tm, d), row), pl.BlockSpec((tm, W), row), pl.BlockSpec((tm, W), row),
                  pl.BlockSpec(wo_bf.shape, fix), pl.BlockSpec((1, d), fix)],
        out_specs=pl.BlockSpec((tm, d), row),
        out_shape=jax.ShapeDtypeStruct((rows, d), F32),
        compiler_params=pltpu.CompilerParams(dimension_semantics=("arbitrary",),
                                             vmem_limit_bytes=VMEM_LIMIT),
        name="back",
    )(x1, cg, dg, wo_bf, g)


def _online_update(s, m_ref, l_ref):
    m_prev = m_ref[:, 0:1]
    m_new = jnp.maximum(m_prev, jnp.max(s, axis=-1, keepdims=True))
    alpha = jnp.exp2(m_prev - m_new)
    p = jnp.exp2(s - m_new)
    l_ref[...] = alpha * l_ref[...] + jnp.sum(p, axis=-1, keepdims=True)
    m_ref[...] = jnp.broadcast_to(m_new, m_ref.shape)
    return alpha, p


def _new_token_mask(s, t_new):
    tok = lax.rem(lax.broadcasted_iota(jnp.int32, s.shape, 0), t_new)
    key = lax.broadcasted_iota(jnp.int32, s.shape, 1)
    return jnp.where(key <= tok, s, -jnp.inf)


def _dec_diff_kernel(pt_ref, q_ref, knt_ref, vn_ref, *refs, pages, t_new):
    kt_refs = refs[:pages]
    v_refs = refs[pages:2 * pages]
    o_ref, m_ref, l_ref, acc_ref = refs[2 * pages:]
    j = pl.program_id(1)

    @pl.when(j == 0)
    def _():
        m_ref[...] = jnp.full(m_ref.shape, NEG_INIT, F32)
        l_ref[...] = jnp.zeros(l_ref.shape, F32)
        acc_ref[...] = jnp.zeros(acc_ref.shape, F32)

    rows = q_ref.shape[1]

    def attend(kts, vals, mask):
        s = jnp.concatenate(
            [jnp.concatenate(
                [jnp.dot(q_ref[h], kt[h * LANES:(h + 1) * LANES, :].astype(BF16),
                         preferred_element_type=F32) for kt in kts], axis=1)
             for h in range(DIFF_H)], axis=0)
        if mask:
            s = _new_token_mask(s, t_new)
        alpha, p = _online_update(s, m_ref, l_ref)
        pb = p.astype(BF16)
        nk = s.shape[1] // len(kts)
        heads = []
        for h in range(DIFF_H):
            pv = None
            for i, val in enumerate(vals):
                t = jnp.dot(pb[h * rows:(h + 1) * rows, i * nk:(i + 1) * nk], val(h).astype(BF16),
                            preferred_element_type=F32)
                pv = t if pv is None else pv + t
            heads.append(pv)
        acc_ref[...] = alpha * acc_ref[...] + jnp.concatenate(heads, axis=0)

    attend(kt_refs, [lambda h, r=r: r[pl.ds(h, PAGE, stride=DIFF_H), :] for r in v_refs], False)

    @pl.when(j == pl.num_programs(1) - 1)
    def _():
        attend([knt_ref], [lambda h: vn_ref[:, h * LANES:(h + 1) * LANES]], True)
        o_ref[...] = acc_ref[...] / l_ref[:, 0:1]


def _dec_diff(page_table, qd, knt, vn, cache_kt, cache_v, *, pages, t_new):
    bd, n_pages = page_table.shape
    rows = qd.shape[2]
    per_b = lambda b, j, pt: (b, 0, 0, 0)
    per_b3 = lambda b, j, pt: (b, 0, 0)

    def page_spec(cache, i):
        return pl.BlockSpec((None,) + cache.shape[1:],
                            lambda b, j, pt: (pt[b, j * pages + i], 0, 0))

    grid_spec = pltpu.PrefetchScalarGridSpec(
        num_scalar_prefetch=1,
        grid=(bd, n_pages // pages),
        in_specs=[pl.BlockSpec((None, DIFF_H, rows, LANES), per_b),
                  pl.BlockSpec((None,) + knt.shape[1:], per_b3),
                  pl.BlockSpec((None,) + vn.shape[1:], per_b3)]
                 + [page_spec(cache_kt, i) for i in range(pages)]
                 + [page_spec(cache_v, i) for i in range(pages)],
        out_specs=pl.BlockSpec((None, DIFF_H * rows, LANES), per_b3),
        scratch_shapes=[pltpu.VMEM((DIFF_H * rows, LANES), F32)] * 3,
    )
    return pl.pallas_call(
        functools.partial(_dec_diff_kernel, pages=pages, t_new=t_new),
        grid_spec=grid_spec,
        out_shape=jax.ShapeDtypeStruct((bd, DIFF_H * rows, LANES), F32),
        compiler_params=pltpu.CompilerParams(dimension_semantics=("arbitrary", "arbitrary"),
                                             vmem_limit_bytes=VMEM_LIMIT),
        name="decode_diff",
    )(page_table, qd, knt, vn, *([cache_kt] * pages), *([cache_v] * pages))


def _dec_mla_kernel(pt_ref, qa_ref, qp_ref, cn_ref, en_ref, *refs, pages, t_new):
    c_refs = refs[:pages]
    e_refs = refs[pages:2 * pages]
    o_ref, m_ref, l_ref, acc_ref = refs[2 * pages:]
    j = pl.program_id(1)

    @pl.when(j == 0)
    def _():
        m_ref[...] = jnp.full(m_ref.shape, NEG_INIT, F32)
        l_ref[...] = jnp.zeros(l_ref.shape, F32)
        acc_ref[...] = jnp.zeros(acc_ref.shape, F32)

    def attend(lat_refs, rope_refs, mask):
        qa = qa_ref[...]
        qp = qp_ref[...]
        lat = [r[...].astype(BF16) for r in lat_refs]
        s = jnp.concatenate(
            [lax.dot_general(qa, c, _NT, preferred_element_type=F32)
             + jnp.dot(qp, et[...].astype(BF16), preferred_element_type=F32)
             for c, et in zip(lat, rope_refs)], axis=1)
        if mask:
            s = _new_token_mask(s, t_new)
        alpha, p = _online_update(s, m_ref, l_ref)
        pb = p.astype(BF16)
        nk = lat[0].shape[0]
        pv = None
        for i, c in enumerate(lat):
            t = jnp.dot(pb[:, i * nk:(i + 1) * nk], c, preferred_element_type=F32)
            pv = t if pv is None else pv + t
        acc_ref[...] = alpha * acc_ref[...] + pv

    attend(c_refs, e_refs, False)

    @pl.when(j == pl.num_programs(1) - 1)
    def _():
        attend([cn_ref], [en_ref], True)
        o_ref[...] = acc_ref[...] / l_ref[:, 0:1]


def _dec_mla(page_table, qa, qp, cn, ent, cache_c, cache_et, *, pages, t_new):
    bd, n_pages = page_table.shape
    rows = qa.shape[1]
    per_b = lambda b, j, pt: (b, 0, 0)
    whole = lambda a: pl.BlockSpec((None,) + a.shape[1:], per_b)

    def page_spec(cache, i):
        return pl.BlockSpec((None,) + cache.shape[1:],
                            lambda b, j, pt: (pt[b, j * pages + i], 0, 0))

    grid_spec = pltpu.PrefetchScalarGridSpec(
        num_scalar_prefetch=1,
        grid=(bd, n_pages // pages),
        in_specs=[whole(qa), whole(qp), whole(cn), whole(ent)]
                 + [page_spec(cache_c, i) for i in range(pages)]
                 + [page_spec(cache_et, i) for i in range(pages)],
        out_specs=pl.BlockSpec((None, rows, KV_LORA), per_b),
        scratch_shapes=[pltpu.VMEM((rows, LANES), F32), pltpu.VMEM((rows, LANES), F32),
                        pltpu.VMEM((rows, KV_LORA), F32)],
    )
    return pl.pallas_call(
        functools.partial(_dec_mla_kernel, pages=pages, t_new=t_new),
        grid_spec=grid_spec,
        out_shape=jax.ShapeDtypeStruct((bd, rows, KV_LORA), F32),
        compiler_params=pltpu.CompilerParams(dimension_semantics=("arbitrary", "arbitrary"),
                                             vmem_limit_bytes=VMEM_LIMIT),
        name="decode_mla",
    )(page_table, qa, qp, cn, ent, *([cache_c] * pages), *([cache_et] * pages))


def _diff_post_kernel(o0_ref, o1_ref, z_ref, lq1_ref, lk1_ref, lq2_ref, lk2_ref, sg_ref, bg_ref,
                      *, lam_init):
    lam = _lam(lq1_ref, lk1_ref, lq2_ref, lk2_ref, lam_init)
    o = o0_ref[...] - lam * o1_ref[...]
    outs = []
    for h in range(DIFF_H):
        outs.append(_rms(o[:, h * DIFF_DV:(h + 1) * DIFF_DV], sg_ref[...]) * (1.0 - lam_init))
    bg_ref[...] = (z_ref[...] * jnp.concatenate(outs, axis=1)).astype(BF16)


def _diff_post(o0, o1, z, lam_params, sg, lam_init):
    return pl.pallas_call(
        functools.partial(_diff_post_kernel, lam_init=lam_init),
        out_shape=jax.ShapeDtypeStruct(o0.shape, BF16),
        name="diff_post",
    )(o0, o1, z, *lam_params, sg)


def _mla_post_kernel(o_ref, z_ref, wuv_ref, dg_ref):
    d = jnp.concatenate(
        [jnp.dot(o_ref[h].astype(BF16), wuv_ref[h], preferred_element_type=F32)
         for h in range(MLA_H)], axis=1)
    dg_ref[...] = (z_ref[...] * d).astype(BF16)


def _mla_post(o, z, wuv_heads_bf):
    return pl.pallas_call(
        _mla_post_kernel,
        out_shape=jax.ShapeDtypeStruct(z.shape, BF16),
        name="mla_post",
    )(o, z, wuv_heads_bf)


def _rope_tables(pos, d, lane0):
    inv = ROPE_THETA ** (-jnp.arange(0, d, 2, dtype=F32) / d)
    ang = pos.astype(F32)[:, None] * inv[None, :]
    cos = jnp.cos(ang)
    sin = jnp.sin(ang)
    cos_d = jnp.concatenate([cos, cos], axis=1)
    sin_d = jnp.concatenate([-sin, sin], axis=1)
    r = pos.shape[0]
    if lane0 == 0 and LANES % d == 0:
        reps = LANES // d
        return jnp.tile(cos_d, (1, reps)), jnp.tile(sin_d, (1, reps))
    pad = LANES - lane0 - d
    cos_t = jnp.concatenate([jnp.ones((r, lane0), F32), cos_d, jnp.ones((r, pad), F32)], axis=1)
    sin_t = jnp.concatenate([jnp.zeros((r, lane0), F32), sin_d, jnp.zeros((r, pad), F32)], axis=1)
    return cos_t, sin_t


def _row_tile(rows, want):
    tm = min(rows, want)
    assert rows % tm == 0 and tm % CONV_ROWS == 0
    return tm


def kernel(x_prompt, x_sample, cache_diff_k, cache_diff_v, state_conf, cache_mla_ckv,
           cache_mla_krope, state_sconv, page_table, norm_g, final_norm_g, w_in_even,
           conf_dw_w, conf_dw_b, conf_ln_g, conf_ln_b, lam_q1, lam_k1, lam_q2, lam_k2,
           diff_subln_g, w_out_even, w_in_odd, sconv_w, mla_q_norm_g, w_uq, mla_kv_norm_g,
           w_uk, w_uv, w_out_odd):
    assert norm_g.shape[0] == 2, "one even and one odd layer"
    B, S, D = x_prompt.shape
    Bd, T, _ = x_sample.shape
    assert B == 1
    n_pages = page_table.shape[1]
    past_len = n_pages * PAGE
    W = BRANCH_W
    conf_k = conf_dw_w.shape[1]
    sconv_k = sconv_w.shape[1]
    lam_init = 0.8 - 0.6 * math.exp(-0.3 * 0)
    row2 = lambda a: a.reshape(1, -1)

    w_even_bf = w_in_even[0].astype(BF16)
    wo_even_bf = w_out_even[0].astype(BF16)
    wo_odd_bf = w_out_odd[0].astype(BF16)
    wi = w_in_odd[0]
    c_kr = 4 * W + Q_LORA + KV_LORA
    kr_slot = jnp.zeros((D, LANES), F32).at[:, ROPE_LANE0:ROPE_LANE0 + MLA_ROPE].set(
        wi[:, c_kr:c_kr + MLA_ROPE])
    w_odd_bf = jnp.concatenate([wi[:, :c_kr], kr_slot, wi[:, c_kr + MLA_ROPE:]], axis=1).astype(BF16)
    qd = MLA_NOPE + MLA_ROPE
    wuq_bf = jnp.pad(w_uq[0].reshape(Q_LORA, MLA_H, qd),
                     ((0, 0), (0, 0), (0, MLA_QSLOT - qd))).reshape(Q_LORA, -1).astype(BF16)
    wuk_slot_bf = jnp.pad(w_uk[0], ((0, 0), (0, 0), (0, MLA_QSLOT - MLA_NOPE))).reshape(
        KV_LORA, -1).astype(BF16)
    wuk_t_bf = jnp.pad(jnp.transpose(w_uk[0], (1, 2, 0)),
                       ((0, 0), (0, MLA_QSLOT - MLA_NOPE), (0, 0))).astype(BF16)
    wuv_bf = w_uv[0].reshape(KV_LORA, -1).astype(BF16)
    wuv_heads_bf = jnp.transpose(w_uv[0], (1, 0, 2)).astype(BF16)
    lam_params = (row2(lam_q1[0]), row2(lam_k1[0]), row2(lam_q2[0]), row2(lam_k2[0]))
    sg = row2(diff_subln_g[0])

    pos_p = jnp.arange(S)
    cos_d, sin_d = _rope_tables(pos_p, DIFF_DK, 0)
    cos_m, sin_m = _rope_tables(pos_p, MLA_ROPE, ROPE_LANE0)
    tm = _row_tile(S, 256)
    tq = min(S, 256)
    tk = min(S, 512)
    xp = x_prompt[0]
    hist_c = jnp.zeros((32, W), F32)
    ag, zb, qb, kf, kb, vf, vb, st_c = _even_front(
        xp, hist_c, row2(norm_g[0]), w_even_bf, cos_d, sin_d, conf_dw_w[0], row2(conf_dw_b[0]),
        row2(conf_ln_g[0]), row2(conf_ln_b[0]), tm=tm, dil=1)
    bg = _flash(functools.partial(_flash_diff_kernel, lam_init=lam_init), qb, kb, vb, zb,
                (*lam_params, sg.reshape(-1, 1)), dk=LANES, tq=tq, tk=tk, name="flash_diff")
    hist_s = jnp.zeros((8, W), F32)
    x1, cg, zd, ckv_p, kpe_p, st_s, qm, km, vm = _mid(
        xp, ag, bg, hist_s, wo_even_bf, row2(norm_g[1]), w_odd_bf, sconv_w[0],
        row2(mla_q_norm_g[0]), wuq_bf, row2(mla_kv_norm_g[0]), wuk_slot_bf, wuv_bf, cos_m, sin_m,
        tm=tm, dil=1, decode=False)
    dg = _flash(_flash_mla_kernel, qm, km, vm, zd, (), dk=2 * MLA_QSLOT, tq=tq, tk=tk,
                name="flash_mla")
    y_p = _back(x1, cg, dg, wo_odd_bf, row2(final_norm_g), tm=tm)

    y_prompt = y_p[None]
    diff_k_prompt = kf.reshape(1, 1, S, DIFF_H, 2, DIFF_DK)
    diff_v_prompt = vf.reshape(1, 1, S, DIFF_H, DIFF_DV)
    conf_state_prompt = st_c[32 - (conf_k - 1):][None, None]
    mla_ckv_prompt = ckv_p[None, None]
    mla_krope_prompt = kpe_p[:, ROPE_LANE0:ROPE_LANE0 + MLA_ROPE][None, None]
    sconv_state_prompt = st_s[8 - (sconv_k - 1):][None, None]

    R = T * Bd
    tmajor = lambda a: jnp.swapaxes(a, 0, 1).reshape((-1,) + a.shape[2:])
    bmajor = lambda a: jnp.swapaxes(a.reshape((-1, Bd) + a.shape[1:]), 0, 1)
    pos_s = jnp.repeat(past_len + jnp.arange(T), Bd)
    cos_d, sin_d = _rope_tables(pos_s, DIFF_DK, 0)
    cos_m, sin_m = _rope_tables(pos_s, MLA_ROPE, ROPE_LANE0)
    xs = tmajor(x_sample)
    ag, zb, qb, kf, kb, vf, vb, st_c = _even_front(
        xs, tmajor(state_conf[0]), row2(norm_g[0]), w_even_bf, cos_d, sin_d, conf_dw_w[0],
        row2(conf_dw_b[0]), row2(conf_ln_g[0]), row2(conf_ln_b[0]), tm=R, dil=Bd)

    rows_d = 16
    q4 = jnp.transpose(qb.reshape(T, Bd, DIFF_H, LANES), (1, 2, 0, 3))
    lane = jnp.arange(LANES)
    q_maps = jnp.concatenate([jnp.where(lane < DIFF_DK, q4, 0), jnp.where(lane >= DIFF_DK, q4, 0)],
                             axis=2)
    q_dec = jnp.pad(q_maps, ((0, 0), (0, 0), (0, rows_d - 2 * T), (0, 0)))
    new_rows = 16
    pad_new = lambda a: jnp.pad(bmajor(a), ((0, 0), (0, new_rows - T), (0, 0)))
    cache_kt = jnp.transpose(cache_diff_k[0], (0, 2, 3, 4, 1)).reshape(-1, W, PAGE)
    cache_v = cache_diff_v[0].reshape(-1, PAGE * DIFF_H, DIFF_DV)
    o_d = _dec_diff(page_table, q_dec, jnp.swapaxes(pad_new(kf), 1, 2), pad_new(vf),
                    cache_kt, cache_v, pages=8, t_new=T)
    o_d = o_d.reshape(Bd, DIFF_H, rows_d, DIFF_DV)
    o_maps = jnp.transpose(o_d[:, :, :2 * T].reshape(Bd, DIFF_H, 2, T, DIFF_DV), (2, 3, 0, 1, 4))
    o_maps = o_maps.reshape(2, R, W)
    bg = _diff_post(o_maps[0], o_maps[1], zb, lam_params, sg, lam_init)

    x1, cg, zd, ckv_s, kpe_s, st_s, qa, qr, _ = _mid(
        xs, ag, bg, tmajor(state_sconv[0]), wo_even_bf, row2(norm_g[1]), w_odd_bf, sconv_w[0],
        row2(mla_q_norm_g[0]), wuq_bf, row2(mla_kv_norm_g[0]), wuk_t_bf, wuv_bf, cos_m, sin_m,
        tm=R, dil=Bd, decode=True)
    qa_dec = jnp.transpose(qa.reshape(T, Bd, MLA_H, KV_LORA), (1, 2, 0, 3)).reshape(
        Bd, MLA_H * T, KV_LORA).astype(BF16)
    qp_dec = jnp.transpose(
        qr.reshape(T, Bd, MLA_H, MLA_QSLOT)[..., ROPE_LANE0:ROPE_LANE0 + MLA_ROPE],
        (1, 2, 0, 3)).reshape(Bd, MLA_H * T, MLA_ROPE).astype(BF16)
    kpe_new = kpe_s[:, ROPE_LANE0:ROPE_LANE0 + MLA_ROPE]
    o_m = _dec_mla(page_table, qa_dec, qp_dec, pad_new(ckv_s), jnp.swapaxes(pad_new(kpe_new), 1, 2),
                   cache_mla_ckv[0], jnp.swapaxes(cache_mla_krope[0], 1, 2),
                   pages=16, t_new=T)
    o_heads = jnp.transpose(o_m.reshape(Bd, MLA_H, T, KV_LORA), (1, 2, 0, 3)).reshape(
        MLA_H, R, KV_LORA)
    dg = _mla_post(o_heads, zd, wuv_heads_bf)
    y_s = _back(x1, cg, dg, wo_odd_bf, row2(final_norm_g), tm=R)

    y_sample = bmajor(y_s)
    diff_k_sample = bmajor(kf).reshape(1, Bd, T, DIFF_H, 2, DIFF_DK)
    diff_v_sample = bmajor(vf).reshape(1, Bd, T, DIFF_H, DIFF_DV)
    keep = lambda prev, new, k: jnp.concatenate([prev[:, new.shape[1]:], new], axis=1)[:, -(k - 1):]
    conf_state_sample = keep(state_conf[0], bmajor(st_c), conf_k)[None]
    mla_ckv_sample = bmajor(ckv_s)[None]
    mla_krope_sample = bmajor(kpe_new)[None]
    sconv_state_sample = keep(state_sconv[0], bmajor(st_s), sconv_k)[None]

    return (y_prompt, y_sample,
            diff_k_prompt, diff_v_prompt, conf_state_prompt,
            mla_ckv_prompt, mla_krope_prompt, sconv_state_prompt,
            diff_k_sample, diff_v_sample, conf_state_sample,
            mla_ckv_sample, mla_krope_sample, sconv_state_sample)
```

```python
import functools
import math

import jax
import jax.numpy as jnp
from jax import lax
from jax.experimental import pallas as pl
from jax.experimental.pallas import tpu as pltpu

F32 = jnp.float32
BF16 = jnp.bfloat16

NORM_EPS = 1e-6
ROPE_THETA = 10000.0
LOG2E = 1.4426950408889634
NEG_INIT = -1e30

LANES = 128
BRANCH_W = 512
DIFF_H, DIFF_DK, DIFF_DV = 4, 64, 128
MLA_H, MLA_NOPE, MLA_ROPE, MLA_DV = 8, 64, 32, 64
Q_LORA = KV_LORA = 256
MLA_QSLOT = 128
ROPE_LANE0 = MLA_NOPE
PAGE = 128

CONV_ROWS = 32
VMEM_LIMIT = 56 * 1024 * 1024

_NT = (((1,), (1,)), ((), ()))


def _rms(x, g):
    return x * lax.rsqrt(jnp.mean(x * x, axis=-1, keepdims=True) + NORM_EPS) * g


def _silu(x):
    return x * jax.nn.sigmoid(x)


def _rope128(x, cos, sin, half):
    lane = lax.broadcasted_iota(jnp.int32, x.shape, 1)
    first = (lane & (2 * half - 1)) < half
    up = pltpu.roll(x, LANES - half, 1)
    dn = pltpu.roll(x, half, 1)
    return x * cos + jnp.where(first, up, dn) * sin


def _rope_wide(x, cos, sin, half):
    n = x.shape[1] // LANES
    return jnp.concatenate(
        [_rope128(x[:, c * LANES:(c + 1) * LANES], cos, sin, half) for c in range(n)], axis=1)


def _resident(shape):
    return pl.BlockSpec(shape, lambda *_: (0,) * len(shape), pipeline_mode=pl.Buffered(1))


def _dw_conv_chunk(buf, w_ref, r0, off, dil, taps):
    acc = None
    for j in range(taps):
        a = r0 + off + j * dil
        t = buf[a:a + CONV_ROWS, :] * w_ref[j:j + 1, :]
        acc = t if acc is None else acc + t
    return acc


def _even_front_kernel(x_ref, hist_ref, g_ref, w_ref, cos_ref, sin_ref, dww_ref, dwb_ref,
                       lng_ref, lnb_ref,
                       ag_ref, zb_ref, qb_ref, kf_ref, kb_ref, vf_ref, vb_ref, st_ref,
                       ubuf, *, tm, hp, off, dil, taps, carry, qscale):
    W = BRANCH_W

    @pl.when(pl.program_id(0) == 0)
    def _():
        ubuf[0:hp, :] = hist_ref[...]

    hb = _rms(x_ref[...], g_ref[...]).astype(BF16)

    def proj(c):
        return jnp.dot(hb, w_ref[:, c * W:(c + 1) * W], preferred_element_type=F32)

    ubuf[hp:hp + tm, :] = proj(0) * jax.nn.sigmoid(proj(1))
    za = proj(2)
    for r0 in range(0, tm, CONV_ROWS):
        c = _dw_conv_chunk(ubuf, dww_ref, r0, off, dil, taps) + dwb_ref[...]
        d = c - jnp.mean(c, axis=-1, keepdims=True)
        y = d * lax.rsqrt(jnp.mean(d * d, axis=-1, keepdims=True) + NORM_EPS)
        y = y * lng_ref[...] + lnb_ref[...]
        ag_ref[r0:r0 + CONV_ROWS, :] = (_silu(za[r0:r0 + CONV_ROWS, :]) * _silu(y)).astype(BF16)
    ns = st_ref.shape[0]
    st_ref[...] = ubuf[hp + tm - ns:hp + tm, :]
    if carry:
        ubuf[0:hp, :] = ubuf[tm:tm + hp, :]

    cos = cos_ref[...]
    sin = sin_ref[...]
    q = _rope_wide(proj(3), cos, sin, DIFF_DK // 2)
    qb_ref[...] = (q * qscale).astype(BF16)
    k = _rope_wide(proj(4), cos, sin, DIFF_DK // 2)
    kf_ref[...] = k
    kb_ref[...] = k.astype(BF16)
    v = proj(5)
    vf_ref[...] = v
    vb_ref[...] = v.astype(BF16)
    zb_ref[...] = _silu(proj(6))


def _even_front(x, hist, g, w_bf, cos, sin, dww, dwb, lng, lnb, *, tm, dil):
    rows, d = x.shape
    taps = dww.shape[0]
    hp = hist.shape[0]
    off = hp - (taps - 1) * dil
    W = BRANCH_W
    n = rows // tm
    row = lambda i: (i, 0)
    fix = lambda i: (0, 0)
    kern = functools.partial(_even_front_kernel, tm=tm, hp=hp, off=off, dil=dil, taps=taps,
                             carry=n > 1, qscale=DIFF_DK ** -0.5 * LOG2E)
    wide = lambda dt: jax.ShapeDtypeStruct((rows, W), dt)
    ns = min(hp, tm)
    return pl.pallas_call(
        kern,
        grid=(n,),
        in_specs=[pl.BlockSpec((tm, d), row), _resident((hp, W)),
                  pl.BlockSpec((1, d), fix), _resident(w_bf.shape),
                  pl.BlockSpec((tm, LANES), row), pl.BlockSpec((tm, LANES), row),
                  pl.BlockSpec((taps, W), fix), pl.BlockSpec((1, W), fix),
                  pl.BlockSpec((1, W), fix), pl.BlockSpec((1, W), fix)],
        out_specs=[pl.BlockSpec((tm, W), row)] * 7 + [pl.BlockSpec((ns, W), fix)],
        out_shape=[wide(BF16), wide(F32), wide(BF16), wide(F32), wide(BF16), wide(F32),
                   wide(BF16), jax.ShapeDtypeStruct((ns, W), F32)],
        scratch_shapes=[pltpu.VMEM((hp + tm, W), F32)],
        compiler_params=pltpu.CompilerParams(dimension_semantics=("arbitrary",),
                                             vmem_limit_bytes=VMEM_LIMIT),
        name="even_front",
    )(x, hist, g, w_bf, cos, sin, dww, dwb, lng, lnb)


def _flash_body(qt_ref, k_ref, vt_ref, q2_ref, s0_ref, s1_ref, m_ref, l_ref, acc_ref, *, tq, tk):
    qi = pl.program_id(1)
    qt = qt_ref[...]
    dk = qt.shape[0]
    feat = lax.broadcasted_iota(jnp.int32, qt.shape, 0)
    zero = jnp.zeros_like(qt)
    q2_ref[...] = jnp.concatenate([jnp.where(feat < dk // 2, qt, zero),
                                   jnp.where(feat >= dk // 2, qt, zero)], axis=1)
    m_ref[...] = jnp.full(m_ref.shape, NEG_INIT, F32)
    l_ref[...] = jnp.zeros(l_ref.shape, F32)
    acc_ref[...] = jnp.zeros(acc_ref.shape, F32)

    def scores(kb, s_ref):
        ks = pl.multiple_of(kb * tk, tk)
        s_ref[...] = jnp.dot(k_ref[pl.ds(ks, tk), :], q2_ref[...],
                             preferred_element_type=F32)

    def accumulate(kb, s_ref, masked):
        st = s_ref[...]
        if masked:
            key = kb * tk + lax.broadcasted_iota(jnp.int32, st.shape, 0)
            qpos = qi * tq + lax.rem(lax.broadcasted_iota(jnp.int32, st.shape, 1), tq)
            st = jnp.where(key <= qpos, st, -jnp.inf)
        m_prev = m_ref[...]
        m_new = jnp.maximum(m_prev, jnp.max(st, axis=0, keepdims=True))
        alpha = jnp.exp2(m_prev - m_new)
        p = jnp.exp2(st - m_new)
        l_ref[...] = alpha * l_ref[...] + jnp.sum(p, axis=0, keepdims=True)
        acc_ref[...] = alpha * acc_ref[...] + jnp.dot(
            vt_ref[kb], p.astype(BF16), preferred_element_type=F32)
        m_ref[...] = m_new

    nfull = (qi * tq) // tk
    scores(0, s0_ref)

    def pair(i, c):
        scores(2 * i + 1, s1_ref)
        accumulate(2 * i, s0_ref, False)
        scores(2 * i + 2, s0_ref)
        accumulate(2 * i + 1, s1_ref, False)
        return c

    lax.fori_loop(0, nfull // 2, pair, 0)
    odd = lax.rem(nfull, 2) == 1

    @pl.when(odd)
    def _():
        scores(nfull, s1_ref)
        accumulate(nfull - 1, s0_ref, False)
        accumulate(nfull, s1_ref, True)

    @pl.when(jnp.logical_not(odd))
    def _():
        accumulate(nfull, s0_ref, True)

    ot = acc_ref[...] / l_ref[...]
    return ot[:, :tq], ot[:, tq:]


def _lam(lq1_ref, lk1_ref, lq2_ref, lk2_ref, lam_init):
    a = jnp.sum(lq1_ref[...] * lk1_ref[...], axis=-1, keepdims=True)
    b = jnp.sum(lq2_ref[...] * lk2_ref[...], axis=-1, keepdims=True)
    return jnp.exp(a) - jnp.exp(b) + lam_init


def _flash_diff_kernel(qt_ref, k_ref, vt_ref, z_ref, lq1_ref, lk1_ref, lq2_ref, lk2_ref, sgt_ref,
                       o_ref, *scratch, tq, tk, lam_init):
    o0, o1 = _flash_body(qt_ref, k_ref, vt_ref, *scratch, tq=tq, tk=tk)
    ot = o0 - _lam(lq1_ref, lk1_ref, lq2_ref, lk2_ref, lam_init) * o1
    ot = ot * lax.rsqrt(jnp.mean(ot * ot, axis=0, keepdims=True) + NORM_EPS)
    ot = ot * sgt_ref[...] * (1.0 - lam_init)
    o_ref[...] = (z_ref[...] * ot.T).astype(BF16)


def _flash_mla_kernel(qt_ref, k_ref, vt_ref, z_ref, o_ref, *scratch, tq, tk):
    o0, o1 = _flash_body(qt_ref, k_ref, vt_ref, *scratch, tq=tq, tk=tk)
    feat = lax.broadcasted_iota(jnp.int32, o0.shape, 0)
    o_ref[...] = (z_ref[...] * jnp.where(feat < MLA_DV, o0, o1).T).astype(BF16)


def _flash(kern, q, k, v, z, extra, *, dk, tq, tk, name):
    S = q.shape[0]
    groups = q.shape[1] // dk
    nk = S // tk
    qt = q.T
    vt = jnp.transpose(v.reshape(nk, tk, groups, LANES), (2, 0, 3, 1))
    fix = lambda g, i: (0, 0)
    return pl.pallas_call(
        functools.partial(kern, tq=tq, tk=tk),
        grid=(groups, S // tq),
        in_specs=[pl.BlockSpec((dk, tq), lambda g, i: (g, i)),
                  pl.BlockSpec((S, dk), lambda g, i: (0, g)),
                  pl.BlockSpec((None, nk, LANES, tk), lambda g, i: (g, 0, 0, 0)),
                  pl.BlockSpec((tq, LANES), lambda g, i: (i, g))]
                 + [pl.BlockSpec(e.shape, fix) for e in extra],
        out_specs=pl.BlockSpec((tq, LANES), lambda g, i: (i, g)),
        out_shape=jax.ShapeDtypeStruct((S, groups * LANES), BF16),
        scratch_shapes=[pltpu.VMEM((dk, 2 * tq), BF16),
                        pltpu.VMEM((tk, 2 * tq), F32), pltpu.VMEM((tk, 2 * tq), F32),
                        pltpu.VMEM((1, 2 * tq), F32), pltpu.VMEM((1, 2 * tq), F32),
                        pltpu.VMEM((LANES, 2 * tq), F32)],
        compiler_params=pltpu.CompilerParams(dimension_semantics=("arbitrary", "arbitrary"),
                                             vmem_limit_bytes=VMEM_LIMIT),
        name=name,
    )(qt, k, vt, z, *extra)


def _mid_kernel(x_ref, ag_ref, bg_ref, hist_ref, wo_ref, g_ref, w_ref, scw_ref, qng_ref, wuq_ref,
                kvg_ref, wuk_ref, wuv_ref, cos_ref, sin_ref,
                x1_ref, cg_ref, zd_ref, ckv_ref, kpe_ref, st_ref, qa_ref, qb_ref, vb_ref,
                sbuf, *, tm, hp, off, dil, taps, carry, qscale, decode):
    W = BRANCH_W

    @pl.when(pl.program_id(0) == 0)
    def _():
        sbuf[0:hp, :] = hist_ref[...]

    x1 = (x_ref[...]
          + jnp.dot(ag_ref[...], wo_ref[0:W, :], preferred_element_type=F32)
          + jnp.dot(bg_ref[...], wo_ref[W:2 * W, :], preferred_element_type=F32))
    x1_ref[...] = x1
    hb = _rms(x1, g_ref[...]).astype(BF16)

    def proj(c0, c1):
        return jnp.dot(hb, w_ref[:, c0:c1], preferred_element_type=F32)

    sbuf[hp:hp + tm, :] = proj(2 * W, 3 * W) * proj(0, W)
    gate = _silu(proj(3 * W, 4 * W)) * proj(W, 2 * W)
    for r0 in range(0, tm, CONV_ROWS):
        conv = _dw_conv_chunk(sbuf, scw_ref, r0, off, dil, taps)
        cg_ref[r0:r0 + CONV_ROWS, :] = (gate[r0:r0 + CONV_ROWS, :] * conv).astype(BF16)
    ns = st_ref.shape[0]
    st_ref[...] = sbuf[hp + tm - ns:hp + tm, :]
    if carry:
        sbuf[0:hp, :] = sbuf[tm:tm + hp, :]

    c0 = 4 * W
    cqn = _rms(proj(c0, c0 + Q_LORA), qng_ref[...]).astype(BF16)
    ckv = _rms(proj(c0 + Q_LORA, c0 + Q_LORA + KV_LORA), kvg_ref[...])
    ckv_ref[...] = ckv
    ckvb = ckv.astype(BF16)
    c1 = c0 + Q_LORA + KV_LORA
    cos = cos_ref[...]
    sin = sin_ref[...]
    kpe = _rope128(proj(c1, c1 + LANES), cos, sin, MLA_ROPE // 2)
    kpe_ref[...] = kpe
    zd_ref[...] = _silu(proj(c1 + LANES, c1 + LANES + W))

    q = jnp.dot(cqn, wuq_ref[...], preferred_element_type=F32)
    q = _rope_wide(q, cos, sin, MLA_ROPE // 2)
    if decode:
        qbf = q.astype(BF16)
        qa_ref[...] = jnp.concatenate(
            [jnp.dot(qbf[:, h * MLA_QSLOT:(h + 1) * MLA_QSLOT], wuk_ref[h],
                     preferred_element_type=F32) for h in range(MLA_H)], axis=1) * qscale
        qb_ref[...] = q * qscale
        vb_ref[...] = jnp.zeros(vb_ref.shape, vb_ref.dtype)
    else:
        qa_ref[...] = (q * qscale).astype(BF16)
        knope = jnp.dot(ckvb, wuk_ref[...], preferred_element_type=F32)
        qb_ref[...] = (knope + jnp.concatenate([kpe] * MLA_H, axis=1)).astype(BF16)
        vb_ref[...] = jnp.dot(ckvb, wuv_ref[...], preferred_element_type=F32).astype(BF16)


def _mid(x, ag, bg, hist, wo_bf, g, w_bf, scw, qng, wuq_bf, kvg, wuk_bf, wuv_bf, cos, sin,
         *, tm, dil, decode):
    rows, d = x.shape
    taps = scw.shape[0]
    hp = hist.shape[0]
    off = hp - (taps - 1) * dil
    W = BRANCH_W
    n = rows // tm
    row = lambda i: (i, 0)
    fix2 = lambda i: (0, 0)
    fix = lambda a: _resident(a.shape)
    ns = min(hp, tm)
    kern = functools.partial(_mid_kernel, tm=tm, hp=hp, off=off, dil=dil, taps=taps,
                             carry=n > 1, qscale=(MLA_NOPE + MLA_ROPE) ** -0.5 * LOG2E,
                             decode=decode)
    qw = MLA_H * MLA_QSLOT
    if decode:
        qa_shape, qa_dt, qb_dt = (rows, MLA_H * KV_LORA), F32, F32
    else:
        qa_shape, qa_dt, qb_dt = (rows, qw), BF16, BF16
    out_shape = [jax.ShapeDtypeStruct((rows, d), F32), jax.ShapeDtypeStruct((rows, W), BF16),
                 jax.ShapeDtypeStruct((rows, W), F32), jax.ShapeDtypeStruct((rows, KV_LORA), F32),
                 jax.ShapeDtypeStruct((rows, LANES), F32), jax.ShapeDtypeStruct((ns, W), F32),
                 jax.ShapeDtypeStruct(qa_shape, qa_dt), jax.ShapeDtypeStruct((rows, qw), qb_dt),
                 jax.ShapeDtypeStruct((rows, W), BF16)]
    out_specs = [pl.BlockSpec((tm, s.shape[1]), row) for s in out_shape]
    out_specs[5] = pl.BlockSpec((ns, W), fix2)
    return pl.pallas_call(
        kern,
        grid=(n,),
        in_specs=[pl.BlockSpec((tm, d), row), pl.BlockSpec((tm, W), row), pl.BlockSpec((tm, W), row),
                  fix(hist), fix(wo_bf), fix(g), fix(w_bf), fix(scw), fix(qng), fix(wuq_bf),
                  fix(kvg), fix(wuk_bf), fix(wuv_bf),
                  pl.BlockSpec((tm, LANES), row), pl.BlockSpec((tm, LANES), row)],
        out_specs=out_specs,
        out_shape=out_shape,
        scratch_shapes=[pltpu.VMEM((hp + tm, W), F32)],
        compiler_params=pltpu.CompilerParams(dimension_semantics=("arbitrary",),
                                             vmem_limit_bytes=VMEM_LIMIT),
        name="mid_decode" if decode else "mid_prompt",
    )(x, ag, bg, hist, wo_bf, g, w_bf, scw, qng, wuq_bf, kvg, wuk_bf, wuv_bf, cos, sin)


def _back_kernel(x_ref, cg_ref, dg_ref, wo_ref, g_ref, y_ref):
    W = BRANCH_W
    x2 = (x_ref[...]
          + jnp.dot(cg_ref[...], wo_ref[0:W, :], preferred_element_type=F32)
          + jnp.dot(dg_ref[...], wo_ref[W:2 * W, :], preferred_element_type=F32))
    y_ref[...] = _rms(x2, g_ref[...])


def _back(x1, cg, dg, wo_bf, g, *, tm):
    rows, d = x1.shape
    W = BRANCH_W
    row = lambda i: (i, 0)
    fix = lambda i: (0, 0)
    return pl.pallas_call(
        _back_kernel,
        grid=(rows // tm,),
        in_specs=[pl.BlockSpec((tm, d), row), pl.BlockSpec((tm, W), row), pl.BlockSpec((tm, W), row),
                  pl.BlockSpec(wo_bf.shape, fix), pl.BlockSpec((1, d), fix)],
        out_specs=pl.BlockSpec((tm, d), row),
        out_shape=jax.ShapeDtypeStruct((rows, d), F32),
        compiler_params=pltpu.CompilerParams(dimension_semantics=("arbitrary",),
                                             vmem_limit_bytes=VMEM_LIMIT),
        name="back",
    )(x1, cg, dg, wo_bf, g)


def _online_update(s, m_ref, l_ref):
    m_prev = m_ref[:, 0:1]
    m_new = jnp.maximum(m_prev, jnp.max(s, axis=-1, keepdims=True))
    alpha = jnp.exp2(m_prev - m_new)
    p = jnp.exp2(s - m_new)
    l_ref[...] = alpha * l_ref[...] + jnp.sum(p, axis=-1, keepdims=True)
    m_ref[...] = jnp.broadcast_to(m_new, m_ref.shape)
    return alpha, p


def _merge_chains(m_ref, l_ref, acc_ref):
    chains = m_ref.shape[0]
    m = m_ref[0]
    for c in range(1, chains):
        m = jnp.maximum(m, m_ref[c])
    l = None
    acc = None
    for c in range(chains):
        w = jnp.exp2(m_ref[c] - m)[:, 0:1]
        l = w * l_ref[c] if l is None else l + w * l_ref[c]
        acc = w * acc_ref[c] if acc is None else acc + w * acc_ref[c]
    return acc / l[:, 0:1]


def _new_token_mask(s, t_new):
    tok = lax.rem(lax.broadcasted_iota(jnp.int32, s.shape, 0), t_new)
    key = lax.broadcasted_iota(jnp.int32, s.shape, 1)
    return jnp.where(key <= tok, s, -jnp.inf)


def _dec_diff_kernel(pt_ref, q_ref, knt_ref, vn_ref, *refs, pages, t_new):
    kt_refs = refs[:pages]
    v_refs = refs[pages:2 * pages]
    o_ref, m_ref, l_ref, acc_ref = refs[2 * pages:]
    j = pl.program_id(1)

    @pl.when(j == 0)
    def _():
        m_ref[...] = jnp.full(m_ref.shape, NEG_INIT, F32)
        l_ref[...] = jnp.zeros(l_ref.shape, F32)
        acc_ref[...] = jnp.zeros(acc_ref.shape, F32)

    rows = q_ref.shape[1]

    def attend(c, kts, vals, mask):
        s = jnp.concatenate(
            [jnp.concatenate(
                [jnp.dot(q_ref[h], kt[h * LANES:(h + 1) * LANES, :].astype(BF16),
                         preferred_element_type=F32) for kt in kts], axis=1)
             for h in range(DIFF_H)], axis=0)
        if mask:
            s = _new_token_mask(s, t_new)
        alpha, p = _online_update(s, m_ref.at[c], l_ref.at[c])
        pb = p.astype(BF16)
        nk = s.shape[1] // len(kts)
        heads = []
        for h in range(DIFF_H):
            pv = None
            for i, val in enumerate(vals):
                t = jnp.dot(pb[h * rows:(h + 1) * rows, i * nk:(i + 1) * nk], val(h).astype(BF16),
                            preferred_element_type=F32)
                pv = t if pv is None else pv + t
            heads.append(pv)
        acc_ref[c] = alpha * acc_ref[c] + jnp.concatenate(heads, axis=0)

    chains = m_ref.shape[0]
    per = pages // chains
    for c in range(chains):
        attend(c, kt_refs[c * per:(c + 1) * per],
               [lambda h, r=r: r[pl.ds(h, PAGE, stride=DIFF_H), :]
                for r in v_refs[c * per:(c + 1) * per]], False)

    @pl.when(j == pl.num_programs(1) - 1)
    def _():
        attend(0, [knt_ref], [lambda h: vn_ref[:, h * LANES:(h + 1) * LANES]], True)
        o_ref[...] = _merge_chains(m_ref, l_ref, acc_ref)


def _page_spec(cache, pages, i):
    return pl.BlockSpec((None,) + cache.shape[1:], lambda b, j, pt: (pt[b, j * pages + i], 0, 0))


def _dec_diff(page_table, qd, knt, vn, cache_kt, cache_v, *, pages, chains, t_new):
    bd, n_pages = page_table.shape
    rows = qd.shape[2]
    per_b = lambda b, j, pt: (b, 0, 0, 0)
    per_b3 = lambda b, j, pt: (b, 0, 0)
    page_spec = lambda cache, i: _page_spec(cache, pages, i)

    grid_spec = pltpu.PrefetchScalarGridSpec(
        num_scalar_prefetch=1,
        grid=(bd, n_pages // pages),
        in_specs=[pl.BlockSpec((None, DIFF_H, rows, LANES), per_b),
                  pl.BlockSpec((None,) + knt.shape[1:], per_b3),
                  pl.BlockSpec((None,) + vn.shape[1:], per_b3)]
                 + [page_spec(cache_kt, i) for i in range(pages)]
                 + [page_spec(cache_v, i) for i in range(pages)],
        out_specs=pl.BlockSpec((None, DIFF_H * rows, LANES), per_b3),
        scratch_shapes=[pltpu.VMEM((chains, DIFF_H * rows, LANES), F32)] * 3,
    )
    return pl.pallas_call(
        functools.partial(_dec_diff_kernel, pages=pages, t_new=t_new),
        grid_spec=grid_spec,
        out_shape=jax.ShapeDtypeStruct((bd, DIFF_H * rows, LANES), F32),
        compiler_params=pltpu.CompilerParams(dimension_semantics=("arbitrary", "arbitrary"),
                                             vmem_limit_bytes=VMEM_LIMIT),
        name="decode_diff",
    )(page_table, qd, knt, vn, *([cache_kt] * pages), *([cache_v] * pages))


def _dec_mla_kernel(pt_ref, qa_ref, qp_ref, cn_ref, en_ref, *refs, pages, t_new):
    c_refs = refs[:pages]
    e_refs = refs[pages:2 * pages]
    o_ref, m_ref, l_ref, acc_ref = refs[2 * pages:]
    j = pl.program_id(1)

    @pl.when(j == 0)
    def _():
        m_ref[...] = jnp.full(m_ref.shape, NEG_INIT, F32)
        l_ref[...] = jnp.zeros(l_ref.shape, F32)
        acc_ref[...] = jnp.zeros(acc_ref.shape, F32)

    def attend(ch, lat_refs, rope_refs, mask):
        qa = qa_ref[...]
        qp = qp_ref[...]
        lat = [r[...].astype(BF16) for r in lat_refs]
        s = jnp.concatenate(
            [lax.dot_general(qa, c, _NT, preferred_element_type=F32)
             + jnp.dot(qp, et[...].astype(BF16), preferred_element_type=F32)
             for c, et in zip(lat, rope_refs)], axis=1)
        if mask:
            s = _new_token_mask(s, t_new)
        alpha, p = _online_update(s, m_ref.at[ch], l_ref.at[ch])
        pb = p.astype(BF16)
        nk = lat[0].shape[0]
        pv = None
        for i, c in enumerate(lat):
            t = jnp.dot(pb[:, i * nk:(i + 1) * nk], c, preferred_element_type=F32)
            pv = t if pv is None else pv + t
        acc_ref[ch] = alpha * acc_ref[ch] + pv

    chains = m_ref.shape[0]
    per = pages // chains
    for ch in range(chains):
        attend(ch, c_refs[ch * per:(ch + 1) * per], e_refs[ch * per:(ch + 1) * per], False)

    @pl.when(j == pl.num_programs(1) - 1)
    def _():
        attend(0, [cn_ref], [en_ref], True)
        o_ref[...] = _merge_chains(m_ref, l_ref, acc_ref)


def _dec_mla(page_table, qa, qp, cn, ent, cache_c, cache_et, *, pages, chains, t_new):
    bd, n_pages = page_table.shape
    rows = qa.shape[1]
    per_b = lambda b, j, pt: (b, 0, 0)
    whole = lambda a: pl.BlockSpec((None,) + a.shape[1:], per_b)
    page_spec = lambda cache, i: _page_spec(cache, pages, i)

    grid_spec = pltpu.PrefetchScalarGridSpec(
        num_scalar_prefetch=1,
        grid=(bd, n_pages // pages),
        in_specs=[whole(qa), whole(qp), whole(cn), whole(ent)]
                 + [page_spec(cache_c, i) for i in range(pages)]
                 + [page_spec(cache_et, i) for i in range(pages)],
        out_specs=pl.BlockSpec((None, rows, KV_LORA), per_b),
        scratch_shapes=[pltpu.VMEM((chains, rows, LANES), F32),
                        pltpu.VMEM((chains, rows, LANES), F32),
                        pltpu.VMEM((chains, rows, KV_LORA), F32)],
    )
    return pl.pallas_call(
        functools.partial(_dec_mla_kernel, pages=pages, t_new=t_new),
        grid_spec=grid_spec,
        out_shape=jax.ShapeDtypeStruct((bd, rows, KV_LORA), F32),
        compiler_params=pltpu.CompilerParams(dimension_semantics=("arbitrary", "arbitrary"),
                                             vmem_limit_bytes=VMEM_LIMIT),
        name="decode_mla",
    )(page_table, qa, qp, cn, ent, *([cache_c] * pages), *([cache_et] * pages))


def _diff_post_kernel(o0_ref, o1_ref, z_ref, lq1_ref, lk1_ref, lq2_ref, lk2_ref, sg_ref, bg_ref,
                      *, lam_init):
    lam = _lam(lq1_ref, lk1_ref, lq2_ref, lk2_ref, lam_init)
    o = o0_ref[...] - lam * o1_ref[...]
    outs = []
    for h in range(DIFF_H):
        outs.append(_rms(o[:, h * DIFF_DV:(h + 1) * DIFF_DV], sg_ref[...]) * (1.0 - lam_init))
    bg_ref[...] = (z_ref[...] * jnp.concatenate(outs, axis=1)).astype(BF16)


def _diff_post(o0, o1, z, lam_params, sg, lam_init):
    return pl.pallas_call(
        functools.partial(_diff_post_kernel, lam_init=lam_init),
        out_shape=jax.ShapeDtypeStruct(o0.shape, BF16),
        name="diff_post",
    )(o0, o1, z, *lam_params, sg)


def _mla_post_kernel(o_ref, z_ref, wuv_ref, dg_ref):
    d = jnp.concatenate(
        [jnp.dot(o_ref[h].astype(BF16), wuv_ref[h], preferred_element_type=F32)
         for h in range(MLA_H)], axis=1)
    dg_ref[...] = (z_ref[...] * d).astype(BF16)


def _mla_post(o, z, wuv_heads_bf):
    return pl.pallas_call(
        _mla_post_kernel,
        out_shape=jax.ShapeDtypeStruct(z.shape, BF16),
        name="mla_post",
    )(o, z, wuv_heads_bf)


def _rope_tables(pos, d, lane0):
    inv = ROPE_THETA ** (-jnp.arange(0, d, 2, dtype=F32) / d)
    ang = pos.astype(F32)[:, None] * inv[None, :]
    cos = jnp.cos(ang)
    sin = jnp.sin(ang)
    cos_d = jnp.concatenate([cos, cos], axis=1)
    sin_d = jnp.concatenate([-sin, sin], axis=1)
    r = pos.shape[0]
    if lane0 == 0 and LANES % d == 0:
        reps = LANES // d
        return jnp.tile(cos_d, (1, reps)), jnp.tile(sin_d, (1, reps))
    pad = LANES - lane0 - d
    cos_t = jnp.concatenate([jnp.ones((r, lane0), F32), cos_d, jnp.ones((r, pad), F32)], axis=1)
    sin_t = jnp.concatenate([jnp.zeros((r, lane0), F32), sin_d, jnp.zeros((r, pad), F32)], axis=1)
    return cos_t, sin_t


def _row_tile(rows, want):
    tm = min(rows, want)
    assert rows % tm == 0 and tm % CONV_ROWS == 0
    return tm


def kernel(x_prompt, x_sample, cache_diff_k, cache_diff_v, state_conf, cache_mla_ckv,
           cache_mla_krope, state_sconv, page_table, norm_g, final_norm_g, w_in_even,
           conf_dw_w, conf_dw_b, conf_ln_g, conf_ln_b, lam_q1, lam_k1, lam_q2, lam_k2,
           diff_subln_g, w_out_even, w_in_odd, sconv_w, mla_q_norm_g, w_uq, mla_kv_norm_g,
           w_uk, w_uv, w_out_odd):
    assert norm_g.shape[0] == 2, "one even and one odd layer"
    B, S, D = x_prompt.shape
    Bd, T, _ = x_sample.shape
    assert B == 1
    n_pages = page_table.shape[1]
    past_len = n_pages * PAGE
    W = BRANCH_W
    conf_k = conf_dw_w.shape[1]
    sconv_k = sconv_w.shape[1]
    lam_init = 0.8 - 0.6 * math.exp(-0.3 * 0)
    row2 = lambda a: a.reshape(1, -1)

    w_even_bf = w_in_even[0].astype(BF16)
    wo_even_bf = w_out_even[0].astype(BF16)
    wo_odd_bf = w_out_odd[0].astype(BF16)
    wi = w_in_odd[0]
    c_kr = 4 * W + Q_LORA + KV_LORA
    kr_slot = jnp.zeros((D, LANES), F32).at[:, ROPE_LANE0:ROPE_LANE0 + MLA_ROPE].set(
        wi[:, c_kr:c_kr + MLA_ROPE])
    w_odd_bf = jnp.concatenate([wi[:, :c_kr], kr_slot, wi[:, c_kr + MLA_ROPE:]], axis=1).astype(BF16)
    qd = MLA_NOPE + MLA_ROPE
    wuq_bf = jnp.pad(w_uq[0].reshape(Q_LORA, MLA_H, qd),
                     ((0, 0), (0, 0), (0, MLA_QSLOT - qd))).reshape(Q_LORA, -1).astype(BF16)
    wuk_slot_bf = jnp.pad(w_uk[0], ((0, 0), (0, 0), (0, MLA_QSLOT - MLA_NOPE))).reshape(
        KV_LORA, -1).astype(BF16)
    wuk_t_bf = jnp.pad(jnp.transpose(w_uk[0], (1, 2, 0)),
                       ((0, 0), (0, MLA_QSLOT - MLA_NOPE), (0, 0))).astype(BF16)
    wuv_bf = w_uv[0].reshape(KV_LORA, -1).astype(BF16)
    wuv_heads_bf = jnp.transpose(w_uv[0], (1, 0, 2)).astype(BF16)
    lam_params = (row2(lam_q1[0]), row2(lam_k1[0]), row2(lam_q2[0]), row2(lam_k2[0]))
    sg = row2(diff_subln_g[0])

    pos_p = jnp.arange(S)
    cos_d, sin_d = _rope_tables(pos_p, DIFF_DK, 0)
    cos_m, sin_m = _rope_tables(pos_p, MLA_ROPE, ROPE_LANE0)
    tm = _row_tile(S, 256)
    tq = min(S, 512)
    tk = min(S, 512)
    xp = x_prompt[0]
    hist_c = jnp.zeros((32, W), F32)
    ag, zb, qb, kf, kb, vf, vb, st_c = _even_front(
        xp, hist_c, row2(norm_g[0]), w_even_bf, cos_d, sin_d, conf_dw_w[0], row2(conf_dw_b[0]),
        row2(conf_ln_g[0]), row2(conf_ln_b[0]), tm=tm, dil=1)
    bg = _flash(functools.partial(_flash_diff_kernel, lam_init=lam_init), qb, kb, vb, zb,
                (*lam_params, sg.reshape(-1, 1)), dk=LANES, tq=tq, tk=tk, name="flash_diff")
    hist_s = jnp.zeros((8, W), F32)
    x1, cg, zd, ckv_p, kpe_p, st_s, qm, km, vm = _mid(
        xp, ag, bg, hist_s, wo_even_bf, row2(norm_g[1]), w_odd_bf, sconv_w[0],
        row2(mla_q_norm_g[0]), wuq_bf, row2(mla_kv_norm_g[0]), wuk_slot_bf, wuv_bf, cos_m, sin_m,
        tm=tm, dil=1, decode=False)
    dg = _flash(_flash_mla_kernel, qm, km, vm, zd, (), dk=2 * MLA_QSLOT, tq=tq, tk=tk,
                name="flash_mla")
    y_p = _back(x1, cg, dg, wo_odd_bf, row2(final_norm_g), tm=tm)

    y_prompt = y_p[None]
    diff_k_prompt = kf.reshape(1, 1, S, DIFF_H, 2, DIFF_DK)
    diff_v_prompt = vf.reshape(1, 1, S, DIFF_H, DIFF_DV)
    conf_state_prompt = st_c[32 - (conf_k - 1):][None, None]
    mla_ckv_prompt = ckv_p[None, None]
    mla_krope_prompt = kpe_p[:, ROPE_LANE0:ROPE_LANE0 + MLA_ROPE][None, None]
    sconv_state_prompt = st_s[8 - (sconv_k - 1):][None, None]

    R = T * Bd
    tmajor = lambda a: jnp.swapaxes(a, 0, 1).reshape((-1,) + a.shape[2:])
    bmajor = lambda a: jnp.swapaxes(a.reshape((-1, Bd) + a.shape[1:]), 0, 1)
    pos_s = jnp.repeat(past_len + jnp.arange(T), Bd)
    cos_d, sin_d = _rope_tables(pos_s, DIFF_DK, 0)
    cos_m, sin_m = _rope_tables(pos_s, MLA_ROPE, ROPE_LANE0)
    xs = tmajor(x_sample)
    ag, zb, qb, kf, kb, vf, vb, st_c = _even_front(
        xs, tmajor(state_conf[0]), row2(norm_g[0]), w_even_bf, cos_d, sin_d, conf_dw_w[0],
        row2(conf_dw_b[0]), row2(conf_ln_g[0]), row2(conf_ln_b[0]), tm=R, dil=Bd)

    rows_d = 16
    q4 = jnp.transpose(qb.reshape(T, Bd, DIFF_H, LANES), (1, 2, 0, 3))
    lane = jnp.arange(LANES)
    q_maps = jnp.concatenate([jnp.where(lane < DIFF_DK, q4, 0), jnp.where(lane >= DIFF_DK, q4, 0)],
                             axis=2)
    q_dec = jnp.pad(q_maps, ((0, 0), (0, 0), (0, rows_d - 2 * T), (0, 0)))
    new_rows = 16
    pad_new = lambda a: jnp.pad(bmajor(a), ((0, 0), (0, new_rows - T), (0, 0)))
    cache_kt = jnp.transpose(cache_diff_k[0], (0, 2, 3, 4, 1)).reshape(-1, W, PAGE)
    cache_v = cache_diff_v[0].reshape(-1, PAGE * DIFF_H, DIFF_DV)
    o_d = _dec_diff(page_table, q_dec, jnp.swapaxes(pad_new(kf), 1, 2), pad_new(vf),
                    cache_kt, cache_v, pages=min(n_pages, 16), chains=2, t_new=T)
    o_d = o_d.reshape(Bd, DIFF_H, rows_d, DIFF_DV)
    o_maps = jnp.transpose(o_d[:, :, :2 * T].reshape(Bd, DIFF_H, 2, T, DIFF_DV), (2, 3, 0, 1, 4))
    o_maps = o_maps.reshape(2, R, W)
    bg = _diff_post(o_maps[0], o_maps[1], zb, lam_params, sg, lam_init)

    x1, cg, zd, ckv_s, kpe_s, st_s, qa, qr, _ = _mid(
        xs, ag, bg, tmajor(state_sconv[0]), wo_even_bf, row2(norm_g[1]), w_odd_bf, sconv_w[0],
        row2(mla_q_norm_g[0]), wuq_bf, row2(mla_kv_norm_g[0]), wuk_t_bf, wuv_bf, cos_m, sin_m,
        tm=R, dil=Bd, decode=True)
    qa_dec = jnp.transpose(qa.reshape(T, Bd, MLA_H, KV_LORA), (1, 2, 0, 3)).reshape(
        Bd, MLA_H * T, KV_LORA).astype(BF16)
    qp_dec = jnp.transpose(
        qr.reshape(T, Bd, MLA_H, MLA_QSLOT)[..., ROPE_LANE0:ROPE_LANE0 + MLA_ROPE],
        (1, 2, 0, 3)).reshape(Bd, MLA_H * T, MLA_ROPE).astype(BF16)
    kpe_new = kpe_s[:, ROPE_LANE0:ROPE_LANE0 + MLA_ROPE]
    o_m = _dec_mla(page_table, qa_dec, qp_dec, pad_new(ckv_s), jnp.swapaxes(pad_new(kpe_new), 1, 2),
                   cache_mla_ckv[0], jnp.swapaxes(cache_mla_krope[0], 1, 2),
                   pages=min(n_pages, 32), chains=1, t_new=T)
    o_heads = jnp.transpose(o_m.reshape(Bd, MLA_H, T, KV_LORA), (1, 2, 0, 3)).reshape(
        MLA_H, R, KV_LORA)
    dg = _mla_post(o_heads, zd, wuv_heads_bf)
    y_s = _back(x1, cg, dg, wo_odd_bf, row2(final_norm_g), tm=R)

    y_sample = bmajor(y_s)
    diff_k_sample = bmajor(kf).reshape(1, Bd, T, DIFF_H, 2, DIFF_DK)
    diff_v_sample = bmajor(vf).reshape(1, Bd, T, DIFF_H, DIFF_DV)
    keep = lambda prev, new, k: jnp.concatenate([prev[:, new.shape[1]:], new], axis=1)[:, -(k - 1):]
    conf_state_sample = keep(state_conf[0], bmajor(st_c), conf_k)[None]
    mla_ckv_sample = bmajor(ckv_s)[None]
    mla_krope_sample = bmajor(kpe_new)[None]
    sconv_state_sample = keep(state_sconv[0], bmajor(st_s), sconv_k)[None]

    return (y_prompt, y_sample,
            diff_k_prompt, diff_v_prompt, conf_state_prompt,
            mla_ckv_prompt, mla_krope_prompt, sconv_state_prompt,
            diff_k_sample, diff_v_sample, conf_state_sample,
            mla_ckv_sample, mla_krope_sample, sconv_state_sample)
```

```python
import functools
import math

import jax
import jax.numpy as jnp
from jax import lax
from jax.experimental import pallas as pl
from jax.experimental.pallas import tpu as pltpu

F32 = jnp.float32
BF16 = jnp.bfloat16

NORM_EPS = 1e-6
ROPE_THETA = 10000.0
LOG2E = 1.4426950408889634
NEG_INIT = -1e30

LANES = 128
BRANCH_W = 512
DIFF_H, DIFF_DK, DIFF_DV = 4, 64, 128
MLA_H, MLA_NOPE, MLA_ROPE, MLA_DV = 8, 64, 32, 64
Q_LORA = KV_LORA = 256
MLA_QSLOT = 128
ROPE_LANE0 = MLA_NOPE
PAGE = 128

CONV_ROWS = 32
VMEM_LIMIT = 56 * 1024 * 1024

_NT = (((1,), (1,)), ((), ()))


def _rms(x, g):
    return x * lax.rsqrt(jnp.mean(x * x, axis=-1, keepdims=True) + NORM_EPS) * g


def _silu(x):
    return x * jax.nn.sigmoid(x)


def _rope128(x, cos, sin, half):
    lane = lax.broadcasted_iota(jnp.int32, x.shape, 1)
    first = (lane & (2 * half - 1)) < half
    up = pltpu.roll(x, LANES - half, 1)
    dn = pltpu.roll(x, half, 1)
    return x * cos + jnp.where(first, up, dn) * sin


def _rope_wide(x, cos, sin, half):
    n = x.shape[1] // LANES
    return jnp.concatenate(
        [_rope128(x[:, c * LANES:(c + 1) * LANES], cos, sin, half) for c in range(n)], axis=1)


def _resident(shape):
    return pl.BlockSpec(shape, lambda *_: (0,) * len(shape), pipeline_mode=pl.Buffered(1))


def _dw_conv_chunk(buf, w_ref, r0, off, dil, taps):
    acc = None
    for j in range(taps):
        a = r0 + off + j * dil
        t = buf[a:a + CONV_ROWS, :] * w_ref[j:j + 1, :]
        acc = t if acc is None else acc + t
    return acc


def _even_front_kernel(x_ref, hist_ref, g_ref, w_ref, cos_ref, sin_ref, dww_ref, dwb_ref,
                       lng_ref, lnb_ref,
                       ag_ref, zb_ref, qb_ref, kf_ref, kb_ref, vf_ref, vb_ref, st_ref,
                       ubuf, *, tm, hp, off, dil, taps, carry, qscale):
    W = BRANCH_W

    @pl.when(pl.program_id(0) == 0)
    def _():
        ubuf[0:hp, :] = hist_ref[...]

    hb = _rms(x_ref[...], g_ref[...]).astype(BF16)

    def proj(c):
        return jnp.dot(hb, w_ref[:, c * W:(c + 1) * W], preferred_element_type=F32)

    ubuf[hp:hp + tm, :] = proj(0) * jax.nn.sigmoid(proj(1))
    za = proj(2)
    for r0 in range(0, tm, CONV_ROWS):
        c = _dw_conv_chunk(ubuf, dww_ref, r0, off, dil, taps) + dwb_ref[...]
        d = c - jnp.mean(c, axis=-1, keepdims=True)
        y = d * lax.rsqrt(jnp.mean(d * d, axis=-1, keepdims=True) + NORM_EPS)
        y = y * lng_ref[...] + lnb_ref[...]
        ag_ref[r0:r0 + CONV_ROWS, :] = (_silu(za[r0:r0 + CONV_ROWS, :]) * _silu(y)).astype(BF16)
    ns = st_ref.shape[0]
    st_ref[...] = ubuf[hp + tm - ns:hp + tm, :]
    if carry:
        ubuf[0:hp, :] = ubuf[tm:tm + hp, :]

    cos = cos_ref[...]
    sin = sin_ref[...]
    q = _rope_wide(proj(3), cos, sin, DIFF_DK // 2)
    qb_ref[...] = (q * qscale).astype(BF16)
    k = _rope_wide(proj(4), cos, sin, DIFF_DK // 2)
    kf_ref[...] = k
    kb_ref[...] = k.astype(BF16)
    v = proj(5)
    vf_ref[...] = v
    vb_ref[...] = v.astype(BF16)
    zb_ref[...] = _silu(proj(6))


def _even_front(x, hist, g, w_bf, cos, sin, dww, dwb, lng, lnb, *, tm, dil):
    rows, d = x.shape
    taps = dww.shape[0]
    hp = hist.shape[0]
    off = hp - (taps - 1) * dil
    W = BRANCH_W
    n = rows // tm
    row = lambda i: (i, 0)
    fix = lambda i: (0, 0)
    kern = functools.partial(_even_front_kernel, tm=tm, hp=hp, off=off, dil=dil, taps=taps,
                             carry=n > 1, qscale=DIFF_DK ** -0.5 * LOG2E)
    wide = lambda dt: jax.ShapeDtypeStruct((rows, W), dt)
    ns = min(hp, tm)
    return pl.pallas_call(
        kern,
        grid=(n,),
        in_specs=[pl.BlockSpec((tm, d), row), _resident((hp, W)),
                  pl.BlockSpec((1, d), fix), _resident(w_bf.shape),
                  pl.BlockSpec((tm, LANES), row), pl.BlockSpec((tm, LANES), row),
                  pl.BlockSpec((taps, W), fix), pl.BlockSpec((1, W), fix),
                  pl.BlockSpec((1, W), fix), pl.BlockSpec((1, W), fix)],
        out_specs=[pl.BlockSpec((tm, W), row)] * 7 + [pl.BlockSpec((ns, W), fix)],
        out_shape=[wide(BF16), wide(F32), wide(BF16), wide(F32), wide(BF16), wide(F32),
                   wide(BF16), jax.ShapeDtypeStruct((ns, W), F32)],
        scratch_shapes=[pltpu.VMEM((hp + tm, W), F32)],
        compiler_params=pltpu.CompilerParams(dimension_semantics=("arbitrary",),
                                             vmem_limit_bytes=VMEM_LIMIT),
        name="even_front",
    )(x, hist, g, w_bf, cos, sin, dww, dwb, lng, lnb)


def _flash_body(qt_ref, k_ref, vt_ref, q2_ref, s0_ref, s1_ref, m_ref, l_ref, acc_ref, *, tq, tk):
    qi = pl.program_id(1)
    qt = qt_ref[...]
    dk = qt.shape[0]
    feat = lax.broadcasted_iota(jnp.int32, qt.shape, 0)
    zero = jnp.zeros_like(qt)
    q2_ref[...] = jnp.concatenate([jnp.where(feat < dk // 2, qt, zero),
                                   jnp.where(feat >= dk // 2, qt, zero)], axis=1)
    m_ref[...] = jnp.full(m_ref.shape, NEG_INIT, F32)
    l_ref[...] = jnp.zeros(l_ref.shape, F32)
    acc_ref[...] = jnp.zeros(acc_ref.shape, F32)

    def scores(kb, s_ref):
        ks = pl.multiple_of(kb * tk, tk)
        s_ref[...] = jnp.dot(k_ref[pl.ds(ks, tk), :], q2_ref[...],
                             preferred_element_type=F32)

    def accumulate(kb, s_ref, masked):
        st = s_ref[...]
        if masked:
            key = kb * tk + lax.broadcasted_iota(jnp.int32, st.shape, 0)
            qpos = qi * tq + lax.rem(lax.broadcasted_iota(jnp.int32, st.shape, 1), tq)
            st = jnp.where(key <= qpos, st, -jnp.inf)
        m_prev = m_ref[...]
        m_new = jnp.maximum(m_prev, jnp.max(st, axis=0, keepdims=True))
        alpha = jnp.exp2(m_prev - m_new)
        p = jnp.exp2(st - m_new)
        l_ref[...] = alpha * l_ref[...] + jnp.sum(p, axis=0, keepdims=True)
        acc_ref[...] = alpha * acc_ref[...] + jnp.dot(
            vt_ref[kb], p.astype(BF16), preferred_element_type=F32)
        m_ref[...] = m_new

    nfull = (qi * tq) // tk
    scores(0, s0_ref)

    def pair(i, c):
        scores(2 * i + 1, s1_ref)
        accumulate(2 * i, s0_ref, False)
        scores(2 * i + 2, s0_ref)
        accumulate(2 * i + 1, s1_ref, False)
        return c

    lax.fori_loop(0, nfull // 2, pair, 0)
    odd = lax.rem(nfull, 2) == 1

    @pl.when(odd)
    def _():
        scores(nfull, s1_ref)
        accumulate(nfull - 1, s0_ref, False)
        accumulate(nfull, s1_ref, True)

    @pl.when(jnp.logical_not(odd))
    def _():
        accumulate(nfull, s0_ref, True)

    ot = acc_ref[...] / l_ref[...]
    return ot[:, :tq], ot[:, tq:]


def _lam(lq1_ref, lk1_ref, lq2_ref, lk2_ref, lam_init):
    a = jnp.sum(lq1_ref[...] * lk1_ref[...], axis=-1, keepdims=True)
    b = jnp.sum(lq2_ref[...] * lk2_ref[...], axis=-1, keepdims=True)
    return jnp.exp(a) - jnp.exp(b) + lam_init


def _flash_diff_kernel(qt_ref, k_ref, vt_ref, z_ref, lq1_ref, lk1_ref, lq2_ref, lk2_ref, sgt_ref,
                       o_ref, *scratch, tq, tk, lam_init):
    o0, o1 = _flash_body(qt_ref, k_ref, vt_ref, *scratch, tq=tq, tk=tk)
    ot = o0 - _lam(lq1_ref, lk1_ref, lq2_ref, lk2_ref, lam_init) * o1
    ot = ot * lax.rsqrt(jnp.mean(ot * ot, axis=0, keepdims=True) + NORM_EPS)
    ot = ot * sgt_ref[...] * (1.0 - lam_init)
    o_ref[...] = (z_ref[...] * ot.T).astype(BF16)


def _flash_mla_kernel(qt_ref, k_ref, vt_ref, z_ref, o_ref, *scratch, tq, tk):
    o0, o1 = _flash_body(qt_ref, k_ref, vt_ref, *scratch, tq=tq, tk=tk)
    feat = lax.broadcasted_iota(jnp.int32, o0.shape, 0)
    o_ref[...] = (z_ref[...] * jnp.where(feat < MLA_DV, o0, o1).T).astype(BF16)


def _flash(kern, q, k, v, z, extra, *, dk, tq, tk, name):
    S = q.shape[0]
    groups = q.shape[1] // dk
    nk = S // tk
    qt = q.T
    vt = jnp.transpose(v.reshape(nk, tk, groups, LANES), (2, 0, 3, 1))
    fix = lambda g, i: (0, 0)
    return pl.pallas_call(
        functools.partial(kern, tq=tq, tk=tk),
        grid=(groups, S // tq),
        in_specs=[pl.BlockSpec((dk, tq), lambda g, i: (g, i)),
                  pl.BlockSpec((S, dk), lambda g, i: (0, g)),
                  pl.BlockSpec((None, nk, LANES, tk), lambda g, i: (g, 0, 0, 0)),
                  pl.BlockSpec((tq, LANES), lambda g, i: (i, g))]
                 + [pl.BlockSpec(e.shape, fix) for e in extra],
        out_specs=pl.BlockSpec((tq, LANES), lambda g, i: (i, g)),
        out_shape=jax.ShapeDtypeStruct((S, groups * LANES), BF16),
        scratch_shapes=[pltpu.VMEM((dk, 2 * tq), BF16),
                        pltpu.VMEM((tk, 2 * tq), F32), pltpu.VMEM((tk, 2 * tq), F32),
                        pltpu.VMEM((1, 2 * tq), F32), pltpu.VMEM((1, 2 * tq), F32),
                        pltpu.VMEM((LANES, 2 * tq), F32)],
        compiler_params=pltpu.CompilerParams(dimension_semantics=("arbitrary", "arbitrary"),
                                             vmem_limit_bytes=VMEM_LIMIT),
        name=name,
    )(qt, k, vt, z, *extra)


def _mid_kernel(x_ref, ag_ref, bg_ref, hist_ref, wo_ref, g_ref, w_ref, scw_ref, qng_ref, wuq_ref,
                kvg_ref, wuk_ref, wuv_ref, cos_ref, sin_ref,
                x1_ref, cg_ref, zd_ref, ckv_ref, kpe_ref, st_ref, qa_ref, qb_ref, vb_ref,
                sbuf, *, tm, hp, off, dil, taps, carry, qscale, decode):
    W = BRANCH_W

    @pl.when(pl.program_id(0) == 0)
    def _():
        sbuf[0:hp, :] = hist_ref[...]

    x1 = (x_ref[...]
          + jnp.dot(ag_ref[...], wo_ref[0:W, :], preferred_element_type=F32)
          + jnp.dot(bg_ref[...], wo_ref[W:2 * W, :], preferred_element_type=F32))
    x1_ref[...] = x1
    hb = _rms(x1, g_ref[...]).astype(BF16)

    def proj(c0, c1):
        return jnp.dot(hb, w_ref[:, c0:c1], preferred_element_type=F32)

    sbuf[hp:hp + tm, :] = proj(2 * W, 3 * W) * proj(0, W)
    gate = _silu(proj(3 * W, 4 * W)) * proj(W, 2 * W)
    for r0 in range(0, tm, CONV_ROWS):
        conv = _dw_conv_chunk(sbuf, scw_ref, r0, off, dil, taps)
        cg_ref[r0:r0 + CONV_ROWS, :] = (gate[r0:r0 + CONV_ROWS, :] * conv).astype(BF16)
    ns = st_ref.shape[0]
    st_ref[...] = sbuf[hp + tm - ns:hp + tm, :]
    if carry:
        sbuf[0:hp, :] = sbuf[tm:tm + hp, :]

    c0 = 4 * W
    cqn = _rms(proj(c0, c0 + Q_LORA), qng_ref[...]).astype(BF16)
    ckv = _rms(proj(c0 + Q_LORA, c0 + Q_LORA + KV_LORA), kvg_ref[...])
    ckv_ref[...] = ckv
    ckvb = ckv.astype(BF16)
    c1 = c0 + Q_LORA + KV_LORA
    cos = cos_ref[...]
    sin = sin_ref[...]
    kpe = _rope128(proj(c1, c1 + LANES), cos, sin, MLA_ROPE // 2)
    kpe_ref[...] = kpe
    zd_ref[...] = _silu(proj(c1 + LANES, c1 + LANES + W))

    q = jnp.dot(cqn, wuq_ref[...], preferred_element_type=F32)
    q = _rope_wide(q, cos, sin, MLA_ROPE // 2)
    if decode:
        qbf = q.astype(BF16)
        qa_ref[...] = jnp.concatenate(
            [jnp.dot(qbf[:, h * MLA_QSLOT:(h + 1) * MLA_QSLOT], wuk_ref[h],
                     preferred_element_type=F32) for h in range(MLA_H)], axis=1) * qscale
        qb_ref[...] = q * qscale
        vb_ref[...] = jnp.zeros(vb_ref.shape, vb_ref.dtype)
    else:
        qa_ref[...] = (q * qscale).astype(BF16)
        knope = jnp.dot(ckvb, wuk_ref[...], preferred_element_type=F32)
        qb_ref[...] = (knope + jnp.concatenate([kpe] * MLA_H, axis=1)).astype(BF16)
        vb_ref[...] = jnp.dot(ckvb, wuv_ref[...], preferred_element_type=F32).astype(BF16)


def _mid(x, ag, bg, hist, wo_bf, g, w_bf, scw, qng, wuq_bf, kvg, wuk_bf, wuv_bf, cos, sin,
         *, tm, dil, decode):
    rows, d = x.shape
    taps = scw.shape[0]
    hp = hist.shape[0]
    off = hp - (taps - 1) * dil
    W = BRANCH_W
    n = rows // tm
    row = lambda i: (i, 0)
    fix2 = lambda i: (0, 0)
    fix = lambda a: _resident(a.shape)
    ns = min(hp, tm)
    kern = functools.partial(_mid_kernel, tm=tm, hp=hp, off=off, dil=dil, taps=taps,
                             carry=n > 1, qscale=(MLA_NOPE + MLA_ROPE) ** -0.5 * LOG2E,
                             decode=decode)
    qw = MLA_H * MLA_QSLOT
    if decode:
        qa_shape, qa_dt, qb_dt = (rows, MLA_H * KV_LORA), F32, F32
    else:
        qa_shape, qa_dt, qb_dt = (rows, qw), BF16, BF16
    out_shape = [jax.ShapeDtypeStruct((rows, d), F32), jax.ShapeDtypeStruct((rows, W), BF16),
                 jax.ShapeDtypeStruct((rows, W), F32), jax.ShapeDtypeStruct((rows, KV_LORA), F32),
                 jax.ShapeDtypeStruct((rows, LANES), F32), jax.ShapeDtypeStruct((ns, W), F32),
                 jax.ShapeDtypeStruct(qa_shape, qa_dt), jax.ShapeDtypeStruct((rows, qw), qb_dt),
                 jax.ShapeDtypeStruct((rows, W), BF16)]
    out_specs = [pl.BlockSpec((tm, s.shape[1]), row) for s in out_shape]
    out_specs[5] = pl.BlockSpec((ns, W), fix2)
    return pl.pallas_call(
        kern,
        grid=(n,),
        in_specs=[pl.BlockSpec((tm, d), row), pl.BlockSpec((tm, W), row), pl.BlockSpec((tm, W), row),
                  fix(hist), fix(wo_bf), fix(g), fix(w_bf), fix(scw), fix(qng), fix(wuq_bf),
                  fix(kvg), fix(wuk_bf), fix(wuv_bf),
                  pl.BlockSpec((tm, LANES), row), pl.BlockSpec((tm, LANES), row)],
        out_specs=out_specs,
        out_shape=out_shape,
        scratch_shapes=[pltpu.VMEM((hp + tm, W), F32)],
        compiler_params=pltpu.CompilerParams(dimension_semantics=("arbitrary",),
                                             vmem_limit_bytes=VMEM_LIMIT),
        name="mid_decode" if decode else "mid_prompt",
    )(x, ag, bg, hist, wo_bf, g, w_bf, scw, qng, wuq_bf, kvg, wuk_bf, wuv_bf, cos, sin)


def _back_kernel(x_ref, cg_ref, dg_ref, wo_ref, g_ref, y_ref):
    W = BRANCH_W
    x2 = (x_ref[...]
          + jnp.dot(cg_ref[...], wo_ref[0:W, :], preferred_element_type=F32)
          + jnp.dot(dg_ref[...], wo_ref[W:2 * W, :], preferred_element_type=F32))
    y_ref[...] = _rms(x2, g_ref[...])


def _back(x1, cg, dg, wo_bf, g, *, tm):
    rows, d = x1.shape
    W = BRANCH_W
    row = lambda i: (i, 0)
    fix = lambda i: (0, 0)
    return pl.pallas_call(
        _back_kernel,
        grid=(rows // tm,),
        in_specs=[pl.BlockSpec((tm, d), row), pl.BlockSpec((tm, W), row), pl.BlockSpec((tm, W), row),
                  pl.BlockSpec(wo_bf.shape, fix), pl.BlockSpec((1, d), fix)],
        out_specs=pl.BlockSpec((tm, d), row),
        out_shape=jax.ShapeDtypeStruct((rows, d), F32),
        compiler_params=pltpu.CompilerParams(dimension_semantics=("arbitrary",),
                                             vmem_limit_bytes=VMEM_LIMIT),
        name="back",
    )(x1, cg, dg, wo_bf, g)


def _online_update(s, m_ref, l_ref):
    m_prev = m_ref[:, 0:1]
    m_new = jnp.maximum(m_prev, jnp.max(s, axis=-1, keepdims=True))
    alpha = jnp.exp2(m_prev - m_new)
    p = jnp.exp2(s - m_new)
    l_ref[...] = alpha * l_ref[...] + jnp.sum(p, axis=-1, keepdims=True)
    m_ref[...] = jnp.broadcast_to(m_new, m_ref.shape)
    return alpha, p


def _merge_chains(m_ref, l_ref, acc_ref):
    chains = m_ref.shape[0]
    m = m_ref[0]
    for c in range(1, chains):
        m = jnp.maximum(m, m_ref[c])
    l = None
    acc = None
    for c in range(chains):
        w = jnp.exp2(m_ref[c] - m)[:, 0:1]
        l = w * l_ref[c] if l is None else l + w * l_ref[c]
        acc = w * acc_ref[c] if acc is None else acc + w * acc_ref[c]
    return acc / l[:, 0:1]


def _new_token_mask(s, t_new):
    tok = lax.rem(lax.broadcasted_iota(jnp.int32, s.shape, 0), t_new)
    key = lax.broadcasted_iota(jnp.int32, s.shape, 1)
    return jnp.where(key <= tok, s, -jnp.inf)


def _dec_diff_kernel(pt_ref, q_ref, knt_ref, vn_ref, *refs, pages, t_new):
    kt_refs = refs[:pages]
    v_refs = refs[pages:2 * pages]
    o_ref, m_ref, l_ref, acc_ref = refs[2 * pages:]
    j = pl.program_id(1)

    @pl.when(j == 0)
    def _():
        m_ref[...] = jnp.full(m_ref.shape, NEG_INIT, F32)
        l_ref[...] = jnp.zeros(l_ref.shape, F32)
        acc_ref[...] = jnp.zeros(acc_ref.shape, F32)

    rows = q_ref.shape[1]

    def attend(c, kts, vals, mask):
        s = jnp.concatenate(
            [jnp.concatenate(
                [jnp.dot(q_ref[h], kt[h * LANES:(h + 1) * LANES, :].astype(BF16),
                         preferred_element_type=F32) for kt in kts], axis=1)
             for h in range(DIFF_H)], axis=0)
        if mask:
            s = _new_token_mask(s, t_new)
        alpha, p = _online_update(s, m_ref.at[c], l_ref.at[c])
        pb = p.astype(BF16)
        nk = s.shape[1] // len(kts)
        heads = []
        for h in range(DIFF_H):
            pv = None
            for i, val in enumerate(vals):
                t = jnp.dot(pb[h * rows:(h + 1) * rows, i * nk:(i + 1) * nk], val(h).astype(BF16),
                            preferred_element_type=F32)
                pv = t if pv is None else pv + t
            heads.append(pv)
        acc_ref[c] = alpha * acc_ref[c] + jnp.concatenate(heads, axis=0)

    chains = m_ref.shape[0]
    per = pages // chains
    for c in range(chains):
        attend(c, kt_refs[c * per:(c + 1) * per],
               [lambda h, r=r: r[pl.ds(h, PAGE, stride=DIFF_H), :]
                for r in v_refs[c * per:(c + 1) * per]], False)

    @pl.when(j == pl.num_programs(1) - 1)
    def _():
        attend(0, [knt_ref], [lambda h: vn_ref[:, h * LANES:(h + 1) * LANES]], True)
        o_ref[...] = _merge_chains(m_ref, l_ref, acc_ref)


def _page_spec(cache, pages, i):
    return pl.BlockSpec((None,) + cache.shape[1:], lambda b, j, pt: (pt[b, j * pages + i], 0, 0))


def _dec_diff(page_table, qd, knt, vn, cache_kt, cache_v, *, pages, chains, t_new):
    bd, n_pages = page_table.shape
    rows = qd.shape[2]
    per_b = lambda b, j, pt: (b, 0, 0, 0)
    per_b3 = lambda b, j, pt: (b, 0, 0)
    page_spec = lambda cache, i: _page_spec(cache, pages, i)

    grid_spec = pltpu.PrefetchScalarGridSpec(
        num_scalar_prefetch=1,
        grid=(bd, n_pages // pages),
        in_specs=[pl.BlockSpec((None, DIFF_H, rows, LANES), per_b),
                  pl.BlockSpec((None,) + knt.shape[1:], per_b3),
                  pl.BlockSpec((None,) + vn.shape[1:], per_b3)]
                 + [page_spec(cache_kt, i) for i in range(pages)]
                 + [page_spec(cache_v, i) for i in range(pages)],
        out_specs=pl.BlockSpec((None, DIFF_H * rows, LANES), per_b3),
        scratch_shapes=[pltpu.VMEM((chains, DIFF_H * rows, LANES), F32)] * 3,
    )
    return pl.pallas_call(
        functools.partial(_dec_diff_kernel, pages=pages, t_new=t_new),
        grid_spec=grid_spec,
        out_shape=jax.ShapeDtypeStruct((bd, DIFF_H * rows, LANES), F32),
        compiler_params=pltpu.CompilerParams(dimension_semantics=("arbitrary", "arbitrary"),
                                             vmem_limit_bytes=VMEM_LIMIT),
        name="decode_diff",
    )(page_table, qd, knt, vn, *([cache_kt] * pages), *([cache_v] * pages))


def _dec_mla_kernel(pt_ref, qa_ref, qp_ref, cn_ref, en_ref, *refs, pages, t_new):
    c_refs = refs[:pages]
    e_refs = refs[pages:2 * pages]
    o_ref, m_ref, l_ref, acc_ref = refs[2 * pages:]
    j = pl.program_id(1)

    @pl.when(j == 0)
    def _():
        m_ref[...] = jnp.full(m_ref.shape, NEG_INIT, F32)
        l_ref[...] = jnp.zeros(l_ref.shape, F32)
        acc_ref[...] = jnp.zeros(acc_ref.shape, F32)

    def attend(ch, lat_refs, rope_refs, mask):
        qa = qa_ref[...]
        qp = qp_ref[...]
        lat = [r[...].astype(BF16) for r in lat_refs]
        s = jnp.concatenate(
            [lax.dot_general(qa, c, _NT, preferred_element_type=F32)
             + jnp.dot(qp, et[...].astype(BF16), preferred_element_type=F32)
             for c, et in zip(lat, rope_refs)], axis=1)
        if mask:
            s = _new_token_mask(s, t_new)
        alpha, p = _online_update(s, m_ref.at[ch], l_ref.at[ch])
        pb = p.astype(BF16)
        nk = lat[0].shape[0]
        pv = None
        for i, c in enumerate(lat):
            t = jnp.dot(pb[:, i * nk:(i + 1) * nk], c, preferred_element_type=F32)
            pv = t if pv is None else pv + t
        acc_ref[ch] = alpha * acc_ref[ch] + pv

    chains = m_ref.shape[0]
    per = pages // chains
    for ch in range(chains):
        attend(ch, c_refs[ch * per:(ch + 1) * per], e_refs[ch * per:(ch + 1) * per], False)

    @pl.when(j == pl.num_programs(1) - 1)
    def _():
        attend(0, [cn_ref], [en_ref], True)
        o_ref[...] = _merge_chains(m_ref, l_ref, acc_ref)


def _dec_mla(page_table, qa, qp, cn, ent, cache_c, cache_et, *, pages, chains, t_new):
    bd, n_pages = page_table.shape
    rows = qa.shape[1]
    per_b = lambda b, j, pt: (b, 0, 0)
    whole = lambda a: pl.BlockSpec((None,) + a.shape[1:], per_b)
    page_spec = lambda cache, i: _page_spec(cache, pages, i)

    grid_spec = pltpu.PrefetchScalarGridSpec(
        num_scalar_prefetch=1,
        grid=(bd, n_pages // pages),
        in_specs=[whole(qa), whole(qp), whole(cn), whole(ent)]
                 + [page_spec(cache_c, i) for i in range(pages)]
                 + [page_spec(cache_et, i) for i in range(pages)],
        out_specs=pl.BlockSpec((None, rows, KV_LORA), per_b),
        scratch_shapes=[pltpu.VMEM((chains, rows, LANES), F32),
                        pltpu.VMEM((chains, rows, LANES), F32),
                        pltpu.VMEM((chains, rows, KV_LORA), F32)],
    )
    return pl.pallas_call(
        functools.partial(_dec_mla_kernel, pages=pages, t_new=t_new),
        grid_spec=grid_spec,
        out_shape=jax.ShapeDtypeStruct((bd, rows, KV_LORA), F32),
        compiler_params=pltpu.CompilerParams(dimension_semantics=("arbitrary", "arbitrary"),
                                             vmem_limit_bytes=VMEM_LIMIT),
        name="decode_mla",
    )(page_table, qa, qp, cn, ent, *([cache_c] * pages), *([cache_et] * pages))


def _diff_post_kernel(o0_ref, o1_ref, z_ref, lq1_ref, lk1_ref, lq2_ref, lk2_ref, sg_ref, bg_ref,
                      *, lam_init):
    lam = _lam(lq1_ref, lk1_ref, lq2_ref, lk2_ref, lam_init)
    o = o0_ref[...] - lam * o1_ref[...]
    outs = []
    for h in range(DIFF_H):
        outs.append(_rms(o[:, h * DIFF_DV:(h + 1) * DIFF_DV], sg_ref[...]) * (1.0 - lam_init))
    bg_ref[...] = (z_ref[...] * jnp.concatenate(outs, axis=1)).astype(BF16)


def _diff_post(o0, o1, z, lam_params, sg, lam_init):
    return pl.pallas_call(
        functools.partial(_diff_post_kernel, lam_init=lam_init),
        out_shape=jax.ShapeDtypeStruct(o0.shape, BF16),
        name="diff_post",
    )(o0, o1, z, *lam_params, sg)


def _mla_post_kernel(o_ref, z_ref, wuv_ref, dg_ref):
    d = jnp.concatenate(
        [jnp.dot(o_ref[h].astype(BF16), wuv_ref[h], preferred_element_type=F32)
         for h in range(MLA_H)], axis=1)
    dg_ref[...] = (z_ref[...] * d).astype(BF16)


def _mla_post(o, z, wuv_heads_bf):
    return pl.pallas_call(
        _mla_post_kernel,
        out_shape=jax.ShapeDtypeStruct(z.shape, BF16),
        name="mla_post",
    )(o, z, wuv_heads_bf)


def _rope_tables(pos, d, lane0):
    inv = ROPE_THETA ** (-jnp.arange(0, d, 2, dtype=F32) / d)
    ang = pos.astype(F32)[:, None] * inv[None, :]
    cos = jnp.cos(ang)
    sin = jnp.sin(ang)
    cos_d = jnp.concatenate([cos, cos], axis=1)
    sin_d = jnp.concatenate([-sin, sin], axis=1)
    r = pos.shape[0]
    if lane0 == 0 and LANES % d == 0:
        reps = LANES // d
        return jnp.tile(cos_d, (1, reps)), jnp.tile(sin_d, (1, reps))
    pad = LANES - lane0 - d
    cos_t = jnp.concatenate([jnp.ones((r, lane0), F32), cos_d, jnp.ones((r, pad), F32)], axis=1)
    sin_t = jnp.concatenate([jnp.zeros((r, lane0), F32), sin_d, jnp.zeros((r, pad), F32)], axis=1)
    return cos_t, sin_t


def _row_tile(rows, want):
    tm = min(rows, want)
    assert rows % tm == 0 and tm % CONV_ROWS == 0
    return tm


def kernel(x_prompt, x_sample, cache_diff_k, cache_diff_v, state_conf, cache_mla_ckv,
           cache_mla_krope, state_sconv, page_table, norm_g, final_norm_g, w_in_even,
           conf_dw_w, conf_dw_b, conf_ln_g, conf_ln_b, lam_q1, lam_k1, lam_q2, lam_k2,
           diff_subln_g, w_out_even, w_in_odd, sconv_w, mla_q_norm_g, w_uq, mla_kv_norm_g,
           w_uk, w_uv, w_out_odd):
    assert norm_g.shape[0] == 2, "one even and one odd layer"
    B, S, D = x_prompt.shape
    Bd, T, _ = x_sample.shape
    assert B == 1
    n_pages = page_table.shape[1]
    past_len = n_pages * PAGE
    W = BRANCH_W
    conf_k = conf_dw_w.shape[1]
    sconv_k = sconv_w.shape[1]
    lam_init = 0.8 - 0.6 * math.exp(-0.3 * 0)
    row2 = lambda a: a.reshape(1, -1)

    w_even_bf = w_in_even[0].astype(BF16)
    wo_even_bf = w_out_even[0].astype(BF16)
    wo_odd_bf = w_out_odd[0].astype(BF16)
    wi = w_in_odd[0]
    c_kr = 4 * W + Q_LORA + KV_LORA
    kr_slot = jnp.zeros((D, LANES), F32).at[:, ROPE_LANE0:ROPE_LANE0 + MLA_ROPE].set(
        wi[:, c_kr:c_kr + MLA_ROPE])
    w_odd_bf = jnp.concatenate([wi[:, :c_kr], kr_slot, wi[:, c_kr + MLA_ROPE:]], axis=1).astype(BF16)
    qd = MLA_NOPE + MLA_ROPE
    wuq_bf = jnp.pad(w_uq[0].reshape(Q_LORA, MLA_H, qd),
                     ((0, 0), (0, 0), (0, MLA_QSLOT - qd))).reshape(Q_LORA, -1).astype(BF16)
    wuk_slot_bf = jnp.pad(w_uk[0], ((0, 0), (0, 0), (0, MLA_QSLOT - MLA_NOPE))).reshape(
        KV_LORA, -1).astype(BF16)
    wuk_t_bf = jnp.pad(jnp.transpose(w_uk[0], (1, 2, 0)),
                       ((0, 0), (0, MLA_QSLOT - MLA_NOPE), (0, 0))).astype(BF16)
    wuv_bf = w_uv[0].reshape(KV_LORA, -1).astype(BF16)
    wuv_heads_bf = jnp.transpose(w_uv[0], (1, 0, 2)).astype(BF16)
    lam_params = (row2(lam_q1[0]), row2(lam_k1[0]), row2(lam_q2[0]), row2(lam_k2[0]))
    sg = row2(diff_subln_g[0])

    pos_p = jnp.arange(S)
    cos_d, sin_d = _rope_tables(pos_p, DIFF_DK, 0)
    cos_m, sin_m = _rope_tables(pos_p, MLA_ROPE, ROPE_LANE0)
    tm = _row_tile(S, 256)
    tq = min(S, 512)
    tk = min(S, 512)
    xp = x_prompt[0]
    hist_c = jnp.zeros((32, W), F32)
    ag, zb, qb, kf, kb, vf, vb, st_c = _even_front(
        xp, hist_c, row2(norm_g[0]), w_even_bf, cos_d, sin_d, conf_dw_w[0], row2(conf_dw_b[0]),
        row2(conf_ln_g[0]), row2(conf_ln_b[0]), tm=tm, dil=1)
    bg = _flash(functools.partial(_flash_diff_kernel, lam_init=lam_init), qb, kb, vb, zb,
                (*lam_params, sg.reshape(-1, 1)), dk=LANES, tq=tq, tk=tk, name="flash_diff")
    hist_s = jnp.zeros((8, W), F32)
    x1, cg, zd, ckv_p, kpe_p, st_s, qm, km, vm = _mid(
        xp, ag, bg, hist_s, wo_even_bf, row2(norm_g[1]), w_odd_bf, sconv_w[0],
        row2(mla_q_norm_g[0]), wuq_bf, row2(mla_kv_norm_g[0]), wuk_slot_bf, wuv_bf, cos_m, sin_m,
        tm=tm, dil=1, decode=False)
    dg = _flash(_flash_mla_kernel, qm, km, vm, zd, (), dk=2 * MLA_QSLOT, tq=tq, tk=tk,
                name="flash_mla")
    y_p = _back(x1, cg, dg, wo_odd_bf, row2(final_norm_g), tm=tm)

    y_prompt = y_p[None]
    diff_k_prompt = kf.reshape(1, 1, S, DIFF_H, 2, DIFF_DK)
    diff_v_prompt = vf.reshape(1, 1, S, DIFF_H, DIFF_DV)
    conf_state_prompt = st_c[32 - (conf_k - 1):][None, None]
    mla_ckv_prompt = ckv_p[None, None]
    mla_krope_prompt = kpe_p[:, ROPE_LANE0:ROPE_LANE0 + MLA_ROPE][None, None]
    sconv_state_prompt = st_s[8 - (sconv_k - 1):][None, None]

    R = T * Bd
    tmajor = lambda a: jnp.swapaxes(a, 0, 1).reshape((-1,) + a.shape[2:])
    bmajor = lambda a: jnp.swapaxes(a.reshape((-1, Bd) + a.shape[1:]), 0, 1)
    pos_s = jnp.repeat(past_len + jnp.arange(T), Bd)
    cos_d, sin_d = _rope_tables(pos_s, DIFF_DK, 0)
    cos_m, sin_m = _rope_tables(pos_s, MLA_ROPE, ROPE_LANE0)
    xs = tmajor(x_sample)
    ag, zb, qb, kf, kb, vf, vb, st_c = _even_front(
        xs, tmajor(state_conf[0]), row2(norm_g[0]), w_even_bf, cos_d, sin_d, conf_dw_w[0],
        row2(conf_dw_b[0]), row2(conf_ln_g[0]), row2(conf_ln_b[0]), tm=R, dil=Bd)

    rows_d = 16
    q4 = jnp.transpose(qb.reshape(T, Bd, DIFF_H, LANES), (1, 2, 0, 3))
    lane = jnp.arange(LANES)
    q_maps = jnp.concatenate([jnp.where(lane < DIFF_DK, q4, 0), jnp.where(lane >= DIFF_DK, q4, 0)],
                             axis=2)
    q_dec = jnp.pad(q_maps, ((0, 0), (0, 0), (0, rows_d - 2 * T), (0, 0)))
    new_rows = 16
    pad_new = lambda a: jnp.pad(bmajor(a), ((0, 0), (0, new_rows - T), (0, 0)))
    cache_kt = jnp.transpose(cache_diff_k[0], (0, 2, 3, 4, 1)).reshape(-1, W, PAGE)
    cache_v = cache_diff_v[0].reshape(-1, PAGE * DIFF_H, DIFF_DV)
    o_d = _dec_diff(page_table, q_dec, jnp.swapaxes(pad_new(kf), 1, 2), pad_new(vf),
                    cache_kt, cache_v, pages=min(n_pages, 32), chains=2, t_new=T)
    o_d = o_d.reshape(Bd, DIFF_H, rows_d, DIFF_DV)
    o_maps = jnp.transpose(o_d[:, :, :2 * T].reshape(Bd, DIFF_H, 2, T, DIFF_DV), (2, 3, 0, 1, 4))
    o_maps = o_maps.reshape(2, R, W)
    bg = _diff_post(o_maps[0], o_maps[1], zb, lam_params, sg, lam_init)

    x1, cg, zd, ckv_s, kpe_s, st_s, qa, qr, _ = _mid(
        xs, ag, bg, tmajor(state_sconv[0]), wo_even_bf, row2(norm_g[1]), w_odd_bf, sconv_w[0],
        row2(mla_q_norm_g[0]), wuq_bf, row2(mla_kv_norm_g[0]), wuk_t_bf, wuv_bf, cos_m, sin_m,
        tm=R, dil=Bd, decode=True)
    qa_dec = jnp.transpose(qa.reshape(T, Bd, MLA_H, KV_LORA), (1, 2, 0, 3)).reshape(
        Bd, MLA_H * T, KV_LORA).astype(BF16)
    qp_dec = jnp.transpose(
        qr.reshape(T, Bd, MLA_H, MLA_QSLOT)[..., ROPE_LANE0:ROPE_LANE0 + MLA_ROPE],
        (1, 2, 0, 3)).reshape(Bd, MLA_H * T, MLA_ROPE).astype(BF16)
    kpe_new = kpe_s[:, ROPE_LANE0:ROPE_LANE0 + MLA_ROPE]
    o_m = _dec_mla(page_table, qa_dec, qp_dec, pad_new(ckv_s), jnp.swapaxes(pad_new(kpe_new), 1, 2),
                   cache_mla_ckv[0], jnp.swapaxes(cache_mla_krope[0], 1, 2),
                   pages=min(n_pages, 64), chains=1, t_new=T)
    o_heads = jnp.transpose(o_m.reshape(Bd, MLA_H, T, KV_LORA), (1, 2, 0, 3)).reshape(
        MLA_H, R, KV_LORA)
    dg = _mla_post(o_heads, zd, wuv_heads_bf)
    y_s = _back(x1, cg, dg, wo_odd_bf, row2(final_norm_g), tm=R)

    y_sample = bmajor(y_s)
    diff_k_sample = bmajor(kf).reshape(1, Bd, T, DIFF_H, 2, DIFF_DK)
    diff_v_sample = bmajor(vf).reshape(1, Bd, T, DIFF_H, DIFF_DV)
    keep = lambda prev, new, k: jnp.concatenate([prev[:, new.shape[1]:], new], axis=1)[:, -(k - 1):]
    conf_state_sample = keep(state_conf[0], bmajor(st_c), conf_k)[None]
    mla_ckv_sample = bmajor(ckv_s)[None]
    mla_krope_sample = bmajor(kpe_new)[None]
    sconv_state_sample = keep(state_sconv[0], bmajor(st_s), sconv_k)[None]

    return (y_prompt, y_sample,
            diff_k_prompt, diff_v_prompt, conf_state_prompt,
            mla_ckv_prompt, mla_krope_prompt, sconv_state_prompt,
            diff_k_sample, diff_v_sample, conf_state_sample,
            mla_ckv_sample, mla_krope_sample, sconv_state_sample)
```

```python
import functools
import math

import jax
import jax.numpy as jnp
from jax import lax
from jax.experimental import pallas as pl
from jax.experimental.pallas import tpu as pltpu

F32 = jnp.float32
BF16 = jnp.bfloat16

NORM_EPS = 1e-6
ROPE_THETA = 10000.0
LOG2E = 1.4426950408889634
NEG_INIT = -1e30

LANES = 128
BRANCH_W = 512
DIFF_H, DIFF_DK, DIFF_DV = 4, 64, 128
MLA_H, MLA_NOPE, MLA_ROPE, MLA_DV = 8, 64, 32, 64
Q_LORA = KV_LORA = 256
MLA_QSLOT = 128
ROPE_LANE0 = MLA_NOPE
PAGE = 128

CONV_ROWS = 32
VMEM_LIMIT = 56 * 1024 * 1024

_NT = (((1,), (1,)), ((), ()))


def _rms(x, g):
    return x * lax.rsqrt(jnp.mean(x * x, axis=-1, keepdims=True) + NORM_EPS) * g


def _silu(x):
    return x * jax.nn.sigmoid(x)


def _rope128(x, cos, sin, half):
    lane = lax.broadcasted_iota(jnp.int32, x.shape, 1)
    first = (lane & (2 * half - 1)) < half
    up = pltpu.roll(x, LANES - half, 1)
    dn = pltpu.roll(x, half, 1)
    return x * cos + jnp.where(first, up, dn) * sin


def _rope_wide(x, cos, sin, half):
    n = x.shape[1] // LANES
    return jnp.concatenate(
        [_rope128(x[:, c * LANES:(c + 1) * LANES], cos, sin, half) for c in range(n)], axis=1)


def _resident(shape):
    return pl.BlockSpec(shape, lambda *_: (0,) * len(shape), pipeline_mode=pl.Buffered(1))


def _store_flash_operands(qt_ref, vt_ref, q, v):
    qt_ref[...] = q.T.astype(BF16)
    for g in range(vt_ref.shape[0]):
        vt_ref[g] = v[:, g * LANES:(g + 1) * LANES].T.astype(BF16)


def _flash_operand_specs(rows, q_width, groups, tm, tk):
    per = tk // tm
    specs = [pl.BlockSpec((q_width, tm), lambda i: (0, i)),
             pl.BlockSpec((groups, None, LANES, tm), lambda i: (0, i // per, 0, i % per))]
    shapes = [jax.ShapeDtypeStruct((q_width, rows), BF16),
              jax.ShapeDtypeStruct((groups, rows // tk, LANES, tk), BF16)]
    return specs, shapes


def _dw_conv_chunk(buf, w_ref, r0, off, dil, taps):
    acc = None
    for j in range(taps):
        a = r0 + off + j * dil
        t = buf[a:a + CONV_ROWS, :] * w_ref[j:j + 1, :]
        acc = t if acc is None else acc + t
    return acc


def _even_front_kernel(x_ref, hist_ref, g_ref, w_ref, cos_ref, sin_ref, dww_ref, dwb_ref,
                       lng_ref, lnb_ref,
                       ag_ref, zb_ref, qb_ref, kf_ref, kb_ref, vf_ref, vb_ref, st_ref,
                       ubuf, *, tm, hp, off, dil, taps, carry, qscale, flash_layout):
    W = BRANCH_W

    @pl.when(pl.program_id(0) == 0)
    def _():
        ubuf[0:hp, :] = hist_ref[...]

    hb = _rms(x_ref[...], g_ref[...]).astype(BF16)

    def proj(c):
        return jnp.dot(hb, w_ref[:, c * W:(c + 1) * W], preferred_element_type=F32)

    ubuf[hp:hp + tm, :] = proj(0) * jax.nn.sigmoid(proj(1))
    za = proj(2)
    for r0 in range(0, tm, CONV_ROWS):
        c = _dw_conv_chunk(ubuf, dww_ref, r0, off, dil, taps) + dwb_ref[...]
        d = c - jnp.mean(c, axis=-1, keepdims=True)
        y = d * lax.rsqrt(jnp.mean(d * d, axis=-1, keepdims=True) + NORM_EPS)
        y = y * lng_ref[...] + lnb_ref[...]
        ag_ref[r0:r0 + CONV_ROWS, :] = (_silu(za[r0:r0 + CONV_ROWS, :]) * _silu(y)).astype(BF16)
    ns = st_ref.shape[0]
    st_ref[...] = ubuf[hp + tm - ns:hp + tm, :]
    if carry:
        ubuf[0:hp, :] = ubuf[tm:tm + hp, :]

    cos = cos_ref[...]
    sin = sin_ref[...]
    q = _rope_wide(proj(3), cos, sin, DIFF_DK // 2)
    k = _rope_wide(proj(4), cos, sin, DIFF_DK // 2)
    kf_ref[...] = k
    kb_ref[...] = k.astype(BF16)
    v = proj(5)
    vf_ref[...] = v
    if flash_layout:
        _store_flash_operands(qb_ref, vb_ref, q * qscale, v)
    else:
        qb_ref[...] = (q * qscale).astype(BF16)
        vb_ref[...] = v.astype(BF16)
    zb_ref[...] = _silu(proj(6))


def _even_front(x, hist, g, w_bf, cos, sin, dww, dwb, lng, lnb, *, tm, dil, flash_tk=0):
    rows, d = x.shape
    taps = dww.shape[0]
    hp = hist.shape[0]
    off = hp - (taps - 1) * dil
    W = BRANCH_W
    n = rows // tm
    row = lambda i: (i, 0)
    fix = lambda i: (0, 0)
    kern = functools.partial(_even_front_kernel, tm=tm, hp=hp, off=off, dil=dil, taps=taps,
                             carry=n > 1, qscale=DIFF_DK ** -0.5 * LOG2E,
                             flash_layout=flash_tk > 0)
    wide = lambda dt: jax.ShapeDtypeStruct((rows, W), dt)
    ns = min(hp, tm)
    out_specs = [pl.BlockSpec((tm, W), row)] * 7 + [pl.BlockSpec((ns, W), fix)]
    out_shape = [wide(BF16), wide(F32), wide(BF16), wide(F32), wide(BF16), wide(F32),
                 wide(BF16), jax.ShapeDtypeStruct((ns, W), F32)]
    if flash_tk:
        (out_specs[2], out_specs[6]), (out_shape[2], out_shape[6]) = _flash_operand_specs(
            rows, W, DIFF_H, tm, flash_tk)
    return pl.pallas_call(
        kern,
        grid=(n,),
        in_specs=[pl.BlockSpec((tm, d), row), _resident((hp, W)),
                  pl.BlockSpec((1, d), fix), _resident(w_bf.shape),
                  pl.BlockSpec((tm, LANES), row), pl.BlockSpec((tm, LANES), row),
                  pl.BlockSpec((taps, W), fix), pl.BlockSpec((1, W), fix),
                  pl.BlockSpec((1, W), fix), pl.BlockSpec((1, W), fix)],
        out_specs=out_specs,
        out_shape=out_shape,
        scratch_shapes=[pltpu.VMEM((hp + tm, W), F32)],
        compiler_params=pltpu.CompilerParams(dimension_semantics=("arbitrary",),
                                             vmem_limit_bytes=VMEM_LIMIT),
        name="even_front",
    )(x, hist, g, w_bf, cos, sin, dww, dwb, lng, lnb)


def _flash_body(qt_ref, k_ref, vt_ref, q2_ref, s0_ref, s1_ref, m_ref, l_ref, acc_ref, *, tq, tk):
    qi = pl.program_id(1)
    qt = qt_ref[...]
    dk = qt.shape[0]
    feat = lax.broadcasted_iota(jnp.int32, qt.shape, 0)
    zero = jnp.zeros_like(qt)
    q2_ref[...] = jnp.concatenate([jnp.where(feat < dk // 2, qt, zero),
                                   jnp.where(feat >= dk // 2, qt, zero)], axis=1)
    m_ref[...] = jnp.full(m_ref.shape, NEG_INIT, F32)
    l_ref[...] = jnp.zeros(l_ref.shape, F32)
    acc_ref[...] = jnp.zeros(acc_ref.shape, F32)

    def scores(kb, s_ref):
        ks = pl.multiple_of(kb * tk, tk)
        s_ref[...] = jnp.dot(k_ref[pl.ds(ks, tk), :], q2_ref[...],
                             preferred_element_type=F32)

    def accumulate(kb, s_ref, masked):
        st = s_ref[...]
        if masked:
            key = kb * tk + lax.broadcasted_iota(jnp.int32, st.shape, 0)
            qpos = qi * tq + lax.rem(lax.broadcasted_iota(jnp.int32, st.shape, 1), tq)
            st = jnp.where(key <= qpos, st, -jnp.inf)
        m_prev = m_ref[...]
        m_new = jnp.maximum(m_prev, jnp.max(st, axis=0, keepdims=True))
        alpha = jnp.exp2(m_prev - m_new)
        p = jnp.exp2(st - m_new)
        l_ref[...] = alpha * l_ref[...] + jnp.sum(p, axis=0, keepdims=True)
        acc_ref[...] = alpha * acc_ref[...] + jnp.dot(
            vt_ref[kb], p.astype(BF16), preferred_element_type=F32)
        m_ref[...] = m_new

    nfull = (qi * tq) // tk
    scores(0, s0_ref)

    def pair(i, c):
        scores(2 * i + 1, s1_ref)
        accumulate(2 * i, s0_ref, False)
        scores(2 * i + 2, s0_ref)
        accumulate(2 * i + 1, s1_ref, False)
        return c

    lax.fori_loop(0, nfull // 2, pair, 0)
    odd = lax.rem(nfull, 2) == 1

    @pl.when(odd)
    def _():
        scores(nfull, s1_ref)
        accumulate(nfull - 1, s0_ref, False)
        accumulate(nfull, s1_ref, True)

    @pl.when(jnp.logical_not(odd))
    def _():
        accumulate(nfull, s0_ref, True)

    ot = acc_ref[...] / l_ref[...]
    return ot[:, :tq], ot[:, tq:]


def _lam(lq1_ref, lk1_ref, lq2_ref, lk2_ref, lam_init):
    a = jnp.sum(lq1_ref[...] * lk1_ref[...], axis=-1, keepdims=True)
    b = jnp.sum(lq2_ref[...] * lk2_ref[...], axis=-1, keepdims=True)
    return jnp.exp(a) - jnp.exp(b) + lam_init


def _flash_diff_kernel(qt_ref, k_ref, vt_ref, z_ref, lq1_ref, lk1_ref, lq2_ref, lk2_ref, sgt_ref,
                       o_ref, *scratch, tq, tk, lam_init):
    o0, o1 = _flash_body(qt_ref, k_ref, vt_ref, *scratch, tq=tq, tk=tk)
    ot = o0 - _lam(lq1_ref, lk1_ref, lq2_ref, lk2_ref, lam_init) * o1
    ot = ot * lax.rsqrt(jnp.mean(ot * ot, axis=0, keepdims=True) + NORM_EPS)
    ot = ot * sgt_ref[...] * (1.0 - lam_init)
    o_ref[...] = (z_ref[...] * ot.T).astype(BF16)


def _flash_mla_kernel(qt_ref, k_ref, vt_ref, z_ref, o_ref, *scratch, tq, tk):
    o0, o1 = _flash_body(qt_ref, k_ref, vt_ref, *scratch, tq=tq, tk=tk)
    feat = lax.broadcasted_iota(jnp.int32, o0.shape, 0)
    o_ref[...] = (z_ref[...] * jnp.where(feat < MLA_DV, o0, o1).T).astype(BF16)


def _flash(kern, qt, k, vt, z, extra, *, dk, tq, tk, name):
    S = k.shape[0]
    groups = k.shape[1] // dk
    nk = S // tk
    fix = lambda g, i: (0, 0)
    return pl.pallas_call(
        functools.partial(kern, tq=tq, tk=tk),
        grid=(groups, S // tq),
        in_specs=[pl.BlockSpec((dk, tq), lambda g, i: (g, i)),
                  pl.BlockSpec((S, dk), lambda g, i: (0, g)),
                  pl.BlockSpec((None, nk, LANES, tk), lambda g, i: (g, 0, 0, 0)),
                  pl.BlockSpec((tq, LANES), lambda g, i: (i, g))]
                 + [pl.BlockSpec(e.shape, fix) for e in extra],
        out_specs=pl.BlockSpec((tq, LANES), lambda g, i: (i, g)),
        out_shape=jax.ShapeDtypeStruct((S, groups * LANES), BF16),
        scratch_shapes=[pltpu.VMEM((dk, 2 * tq), BF16),
                        pltpu.VMEM((tk, 2 * tq), F32), pltpu.VMEM((tk, 2 * tq), F32),
                        pltpu.VMEM((1, 2 * tq), F32), pltpu.VMEM((1, 2 * tq), F32),
                        pltpu.VMEM((LANES, 2 * tq), F32)],
        compiler_params=pltpu.CompilerParams(dimension_semantics=("arbitrary", "arbitrary"),
                                             vmem_limit_bytes=VMEM_LIMIT),
        name=name,
    )(qt, k, vt, z, *extra)


def _mid_kernel(x_ref, ag_ref, bg_ref, hist_ref, wo_ref, g_ref, w_ref, scw_ref, qng_ref, wuq_ref,
                kvg_ref, wuk_ref, wuv_ref, cos_ref, sin_ref,
                x1_ref, cg_ref, zd_ref, ckv_ref, kpe_ref, st_ref, qa_ref, qb_ref, vb_ref,
                sbuf, *, tm, hp, off, dil, taps, carry, qscale, decode):
    W = BRANCH_W

    @pl.when(pl.program_id(0) == 0)
    def _():
        sbuf[0:hp, :] = hist_ref[...]

    x1 = (x_ref[...]
          + jnp.dot(ag_ref[...], wo_ref[0:W, :], preferred_element_type=F32)
          + jnp.dot(bg_ref[...], wo_ref[W:2 * W, :], preferred_element_type=F32))
    x1_ref[...] = x1
    hb = _rms(x1, g_ref[...]).astype(BF16)

    def proj(c0, c1):
        return jnp.dot(hb, w_ref[:, c0:c1], preferred_element_type=F32)

    sbuf[hp:hp + tm, :] = proj(2 * W, 3 * W) * proj(0, W)
    gate = _silu(proj(3 * W, 4 * W)) * proj(W, 2 * W)
    for r0 in range(0, tm, CONV_ROWS):
        conv = _dw_conv_chunk(sbuf, scw_ref, r0, off, dil, taps)
        cg_ref[r0:r0 + CONV_ROWS, :] = (gate[r0:r0 + CONV_ROWS, :] * conv).astype(BF16)
    ns = st_ref.shape[0]
    st_ref[...] = sbuf[hp + tm - ns:hp + tm, :]
    if carry:
        sbuf[0:hp, :] = sbuf[tm:tm + hp, :]

    c0 = 4 * W
    cqn = _rms(proj(c0, c0 + Q_LORA), qng_ref[...]).astype(BF16)
    ckv = _rms(proj(c0 + Q_LORA, c0 + Q_LORA + KV_LORA), kvg_ref[...])
    ckv_ref[...] = ckv
    ckvb = ckv.astype(BF16)
    c1 = c0 + Q_LORA + KV_LORA
    cos = cos_ref[...]
    sin = sin_ref[...]
    kpe = _rope128(proj(c1, c1 + LANES), cos, sin, MLA_ROPE // 2)
    kpe_ref[...] = kpe
    zd_ref[...] = _silu(proj(c1 + LANES, c1 + LANES + W))

    q = jnp.dot(cqn, wuq_ref[...], preferred_element_type=F32)
    q = _rope_wide(q, cos, sin, MLA_ROPE // 2)
    if decode:
        qbf = q.astype(BF16)
        qa_ref[...] = jnp.concatenate(
            [jnp.dot(qbf[:, h * MLA_QSLOT:(h + 1) * MLA_QSLOT], wuk_ref[h],
                     preferred_element_type=F32) for h in range(MLA_H)], axis=1) * qscale
        qb_ref[...] = q * qscale
        vb_ref[...] = jnp.zeros(vb_ref.shape, vb_ref.dtype)
    else:
        knope = jnp.dot(ckvb, wuk_ref[...], preferred_element_type=F32)
        qb_ref[...] = (knope + jnp.concatenate([kpe] * MLA_H, axis=1)).astype(BF16)
        _store_flash_operands(qa_ref, vb_ref, q * qscale,
                              jnp.dot(ckvb, wuv_ref[...], preferred_element_type=F32))


def _mid(x, ag, bg, hist, wo_bf, g, w_bf, scw, qng, wuq_bf, kvg, wuk_bf, wuv_bf, cos, sin,
         *, tm, dil, decode, flash_tk=0):
    rows, d = x.shape
    taps = scw.shape[0]
    hp = hist.shape[0]
    off = hp - (taps - 1) * dil
    W = BRANCH_W
    n = rows // tm
    row = lambda i: (i, 0)
    fix2 = lambda i: (0, 0)
    fix = lambda a: _resident(a.shape)
    ns = min(hp, tm)
    kern = functools.partial(_mid_kernel, tm=tm, hp=hp, off=off, dil=dil, taps=taps,
                             carry=n > 1, qscale=(MLA_NOPE + MLA_ROPE) ** -0.5 * LOG2E,
                             decode=decode)
    qw = MLA_H * MLA_QSLOT
    if decode:
        qa_shape, qa_dt, qb_dt = (rows, MLA_H * KV_LORA), F32, F32
    else:
        qa_shape, qa_dt, qb_dt = (rows, qw), BF16, BF16
    out_shape = [jax.ShapeDtypeStruct((rows, d), F32), jax.ShapeDtypeStruct((rows, W), BF16),
                 jax.ShapeDtypeStruct((rows, W), F32), jax.ShapeDtypeStruct((rows, KV_LORA), F32),
                 jax.ShapeDtypeStruct((rows, LANES), F32), jax.ShapeDtypeStruct((ns, W), F32),
                 jax.ShapeDtypeStruct(qa_shape, qa_dt), jax.ShapeDtypeStruct((rows, qw), qb_dt),
                 jax.ShapeDtypeStruct((rows, W), BF16)]
    out_specs = [pl.BlockSpec((tm, s.shape[1]), row) for s in out_shape]
    out_specs[5] = pl.BlockSpec((ns, W), fix2)
    if not decode:
        (out_specs[6], out_specs[8]), (out_shape[6], out_shape[8]) = _flash_operand_specs(
            rows, qw, W // LANES, tm, flash_tk)
    return pl.pallas_call(
        kern,
        grid=(n,),
        in_specs=[pl.BlockSpec((tm, d), row), pl.BlockSpec((tm, W), row), pl.BlockSpec((tm, W), row),
                  fix(hist), fix(wo_bf), fix(g), fix(w_bf), fix(scw), fix(qng), fix(wuq_bf),
                  fix(kvg), fix(wuk_bf), fix(wuv_bf),
                  pl.BlockSpec((tm, LANES), row), pl.BlockSpec((tm, LANES), row)],
        out_specs=out_specs,
        out_shape=out_shape,
        scratch_shapes=[pltpu.VMEM((hp + tm, W), F32)],
        compiler_params=pltpu.CompilerParams(dimension_semantics=("arbitrary",),
                                             vmem_limit_bytes=VMEM_LIMIT),
        name="mid_decode" if decode else "mid_prompt",
    )(x, ag, bg, hist, wo_bf, g, w_bf, scw, qng, wuq_bf, kvg, wuk_bf, wuv_bf, cos, sin)


def _back_kernel(x_ref, cg_ref, dg_ref, wo_ref, g_ref, y_ref):
    W = BRANCH_W
    x2 = (x_ref[...]
          + jnp.dot(cg_ref[...], wo_ref[0:W, :], preferred_element_type=F32)
          + jnp.dot(dg_ref[...], wo_ref[W:2 * W, :], preferred_element_type=F32))
    y_ref[...] = _rms(x2, g_ref[...])


def _back(x1, cg, dg, wo_bf, g, *, tm):
    rows, d = x1.shape
    W = BRANCH_W
    row = lambda i: (i, 0)
    fix = lambda i: (0, 0)
    return pl.pallas_call(
        _back_kernel,
        grid=(rows // tm,),
        in_specs=[pl.BlockSpec((tm, d), row), pl.BlockSpec((tm, W), row), pl.BlockSpec((tm, W), row),
                  pl.BlockSpec(wo_bf.shape, fix), pl.BlockSpec((1, d), fix)],
        out_specs=pl.BlockSpec((tm, d), row),
        out_shape=jax.ShapeDtypeStruct((rows, d), F32),
        compiler_params=pltpu.CompilerParams(dimension_semantics=("arbitrary",),
                                             vmem_limit_bytes=VMEM_LIMIT),
        name="back",
    )(x1, cg, dg, wo_bf, g)


def _online_update(s, m_ref, l_ref):
    m_prev = m_ref[:, 0:1]
    m_new = jnp.maximum(m_prev, jnp.max(s, axis=-1, keepdims=True))
    alpha = jnp.exp2(m_prev - m_new)
    p = jnp.exp2(s - m_new)
    l_ref[...] = alpha * l_ref[...] + jnp.sum(p, axis=-1, keepdims=True)
    m_ref[...] = jnp.broadcast_to(m_new, m_ref.shape)
    return alpha, p


def _merge_chains(m_ref, l_ref, acc_ref):
    chains = m_ref.shape[0]
    m = m_ref[0]
    for c in range(1, chains):
        m = jnp.maximum(m, m_ref[c])
    l = None
    acc = None
    for c in range(chains):
        w = jnp.exp2(m_ref[c] - m)[:, 0:1]
        l = w * l_ref[c] if l is None else l + w * l_ref[c]
        acc = w * acc_ref[c] if acc is None else acc + w * acc_ref[c]
    return acc / l[:, 0:1]


def _new_token_mask(s, t_new):
    tok = lax.rem(lax.broadcasted_iota(jnp.int32, s.shape, 0), t_new)
    key = lax.broadcasted_iota(jnp.int32, s.shape, 1)
    return jnp.where(key <= tok, s, -jnp.inf)


def _dec_diff_kernel(pt_ref, q_ref, knt_ref, vn_ref, *refs, pages, t_new):
    kt_refs = refs[:pages]
    v_refs = refs[pages:2 * pages]
    o_ref, m_ref, l_ref, acc_ref = refs[2 * pages:]
    j = pl.program_id(1)

    @pl.when(j == 0)
    def _():
        m_ref[...] = jnp.full(m_ref.shape, NEG_INIT, F32)
        l_ref[...] = jnp.zeros(l_ref.shape, F32)
        acc_ref[...] = jnp.zeros(acc_ref.shape, F32)

    rows = q_ref.shape[1]

    def attend(c, kts, vals, mask):
        s = jnp.concatenate(
            [jnp.concatenate(
                [jnp.dot(q_ref[h], kt[h * LANES:(h + 1) * LANES, :].astype(BF16),
                         preferred_element_type=F32) for kt in kts], axis=1)
             for h in range(DIFF_H)], axis=0)
        if mask:
            s = _new_token_mask(s, t_new)
        alpha, p = _online_update(s, m_ref.at[c], l_ref.at[c])
        pb = p.astype(BF16)
        nk = s.shape[1] // len(kts)
        heads = []
        for h in range(DIFF_H):
            pv = None
            for i, val in enumerate(vals):
                t = jnp.dot(pb[h * rows:(h + 1) * rows, i * nk:(i + 1) * nk], val(h).astype(BF16),
                            preferred_element_type=F32)
                pv = t if pv is None else pv + t
            heads.append(pv)
        acc_ref[c] = alpha * acc_ref[c] + jnp.concatenate(heads, axis=0)

    chains = m_ref.shape[0]
    per = pages // chains
    for c in range(chains):
        attend(c, kt_refs[c * per:(c + 1) * per],
               [lambda h, r=r: r[pl.ds(h, PAGE, stride=DIFF_H), :]
                for r in v_refs[c * per:(c + 1) * per]], False)

    @pl.when(j == pl.num_programs(1) - 1)
    def _():
        attend(0, [knt_ref], [lambda h: vn_ref[:, h * LANES:(h + 1) * LANES]], True)
        o_ref[...] = _merge_chains(m_ref, l_ref, acc_ref)


def _page_spec(cache, pages, i):
    return pl.BlockSpec((None,) + cache.shape[1:], lambda b, j, pt: (pt[b, j * pages + i], 0, 0))


def _dec_diff(page_table, qd, knt, vn, cache_kt, cache_v, *, pages, chains, t_new):
    bd, n_pages = page_table.shape
    rows = qd.shape[2]
    per_b = lambda b, j, pt: (b, 0, 0, 0)
    per_b3 = lambda b, j, pt: (b, 0, 0)
    page_spec = lambda cache, i: _page_spec(cache, pages, i)

    grid_spec = pltpu.PrefetchScalarGridSpec(
        num_scalar_prefetch=1,
        grid=(bd, n_pages // pages),
        in_specs=[pl.BlockSpec((None, DIFF_H, rows, LANES), per_b),
                  pl.BlockSpec((None,) + knt.shape[1:], per_b3),
                  pl.BlockSpec((None,) + vn.shape[1:], per_b3)]
                 + [page_spec(cache_kt, i) for i in range(pages)]
                 + [page_spec(cache_v, i) for i in range(pages)],
        out_specs=pl.BlockSpec((None, DIFF_H * rows, LANES), per_b3),
        scratch_shapes=[pltpu.VMEM((chains, DIFF_H * rows, LANES), F32)] * 3,
    )
    return pl.pallas_call(
        functools.partial(_dec_diff_kernel, pages=pages, t_new=t_new),
        grid_spec=grid_spec,
        out_shape=jax.ShapeDtypeStruct((bd, DIFF_H * rows, LANES), F32),
        compiler_params=pltpu.CompilerParams(dimension_semantics=("arbitrary", "arbitrary"),
                                             vmem_limit_bytes=VMEM_LIMIT),
        name="decode_diff",
    )(page_table, qd, knt, vn, *([cache_kt] * pages), *([cache_v] * pages))


def _dec_mla_kernel(pt_ref, qa_ref, qp_ref, cn_ref, en_ref, *refs, pages, t_new):
    c_refs = refs[:pages]
    e_refs = refs[pages:2 * pages]
    o_ref, m_ref, l_ref, acc_ref = refs[2 * pages:]
    j = pl.program_id(1)

    @pl.when(j == 0)
    def _():
        m_ref[...] = jnp.full(m_ref.shape, NEG_INIT, F32)
        l_ref[...] = jnp.zeros(l_ref.shape, F32)
        acc_ref[...] = jnp.zeros(acc_ref.shape, F32)

    def attend(ch, lat_refs, rope_refs, mask):
        qa = qa_ref[...]
        qp = qp_ref[...]
        lat = [r[...].astype(BF16) for r in lat_refs]
        s = jnp.concatenate(
            [lax.dot_general(qa, c, _NT, preferred_element_type=F32)
             + jnp.dot(qp, et[...].astype(BF16), preferred_element_type=F32)
             for c, et in zip(lat, rope_refs)], axis=1)
        if mask:
            s = _new_token_mask(s, t_new)
        alpha, p = _online_update(s, m_ref.at[ch], l_ref.at[ch])
        pb = p.astype(BF16)
        nk = lat[0].shape[0]
        pv = None
        for i, c in enumerate(lat):
            t = jnp.dot(pb[:, i * nk:(i + 1) * nk], c, preferred_element_type=F32)
            pv = t if pv is None else pv + t
        acc_ref[ch] = alpha * acc_ref[ch] + pv

    chains = m_ref.shape[0]
    per = pages // chains
    for ch in range(chains):
        attend(ch, c_refs[ch * per:(ch + 1) * per], e_refs[ch * per:(ch + 1) * per], False)

    @pl.when(j == pl.num_programs(1) - 1)
    def _():
        attend(0, [cn_ref], [en_ref], True)
        o_ref[...] = _merge_chains(m_ref, l_ref, acc_ref)


def _dec_mla(page_table, qa, qp, cn, ent, cache_c, cache_et, *, pages, chains, t_new):
    bd, n_pages = page_table.shape
    rows = qa.shape[1]
    per_b = lambda b, j, pt: (b, 0, 0)
    whole = lambda a: pl.BlockSpec((None,) + a.shape[1:], per_b)
    page_spec = lambda cache, i: _page_spec(cache, pages, i)

    grid_spec = pltpu.PrefetchScalarGridSpec(
        num_scalar_prefetch=1,
        grid=(bd, n_pages // pages),
        in_specs=[whole(qa), whole(qp), whole(cn), whole(ent)]
                 + [page_spec(cache_c, i) for i in range(pages)]
                 + [page_spec(cache_et, i) for i in range(pages)],
        out_specs=pl.BlockSpec((None, rows, KV_LORA), per_b),
        scratch_shapes=[pltpu.VMEM((chains, rows, LANES), F32),
                        pltpu.VMEM((chains, rows, LANES), F32),
                        pltpu.VMEM((chains, rows, KV_LORA), F32)],
    )
    return pl.pallas_call(
        functools.partial(_dec_mla_kernel, pages=pages, t_new=t_new),
        grid_spec=grid_spec,
        out_shape=jax.ShapeDtypeStruct((bd, rows, KV_LORA), F32),
        compiler_params=pltpu.CompilerParams(dimension_semantics=("arbitrary", "arbitrary"),
                                             vmem_limit_bytes=VMEM_LIMIT),
        name="decode_mla",
    )(page_table, qa, qp, cn, ent, *([cache_c] * pages), *([cache_et] * pages))


def _diff_post_kernel(o0_ref, o1_ref, z_ref, lq1_ref, lk1_ref, lq2_ref, lk2_ref, sg_ref, bg_ref,
                      *, lam_init):
    lam = _lam(lq1_ref, lk1_ref, lq2_ref, lk2_ref, lam_init)
    o = o0_ref[...] - lam * o1_ref[...]
    outs = []
    for h in range(DIFF_H):
        outs.append(_rms(o[:, h * DIFF_DV:(h + 1) * DIFF_DV], sg_ref[...]) * (1.0 - lam_init))
    bg_ref[...] = (z_ref[...] * jnp.concatenate(outs, axis=1)).astype(BF16)


def _diff_post(o0, o1, z, lam_params, sg, lam_init):
    return pl.pallas_call(
        functools.partial(_diff_post_kernel, lam_init=lam_init),
        out_shape=jax.ShapeDtypeStruct(o0.shape, BF16),
        name="diff_post",
    )(o0, o1, z, *lam_params, sg)


def _mla_post_kernel(o_ref, z_ref, wuv_ref, dg_ref):
    d = jnp.concatenate(
        [jnp.dot(o_ref[h].astype(BF16), wuv_ref[h], preferred_element_type=F32)
         for h in range(MLA_H)], axis=1)
    dg_ref[...] = (z_ref[...] * d).astype(BF16)


def _mla_post(o, z, wuv_heads_bf):
    return pl.pallas_call(
        _mla_post_kernel,
        out_shape=jax.ShapeDtypeStruct(z.shape, BF16),
        name="mla_post",
    )(o, z, wuv_heads_bf)


def _rope_tables(pos, d, lane0):
    inv = ROPE_THETA ** (-jnp.arange(0, d, 2, dtype=F32) / d)
    ang = pos.astype(F32)[:, None] * inv[None, :]
    cos = jnp.cos(ang)
    sin = jnp.sin(ang)
    cos_d = jnp.concatenate([cos, cos], axis=1)
    sin_d = jnp.concatenate([-sin, sin], axis=1)
    r = pos.shape[0]
    if lane0 == 0 and LANES % d == 0:
        reps = LANES // d
        return jnp.tile(cos_d, (1, reps)), jnp.tile(sin_d, (1, reps))
    pad = LANES - lane0 - d
    cos_t = jnp.concatenate([jnp.ones((r, lane0), F32), cos_d, jnp.ones((r, pad), F32)], axis=1)
    sin_t = jnp.concatenate([jnp.zeros((r, lane0), F32), sin_d, jnp.zeros((r, pad), F32)], axis=1)
    return cos_t, sin_t


def _row_tile(rows, want):
    tm = min(rows, want)
    assert rows % tm == 0 and tm % CONV_ROWS == 0
    return tm


def kernel(x_prompt, x_sample, cache_diff_k, cache_diff_v, state_conf, cache_mla_ckv,
           cache_mla_krope, state_sconv, page_table, norm_g, final_norm_g, w_in_even,
           conf_dw_w, conf_dw_b, conf_ln_g, conf_ln_b, lam_q1, lam_k1, lam_q2, lam_k2,
           diff_subln_g, w_out_even, w_in_odd, sconv_w, mla_q_norm_g, w_uq, mla_kv_norm_g,
           w_uk, w_uv, w_out_odd):
    assert norm_g.shape[0] == 2, "one even and one odd layer"
    B, S, D = x_prompt.shape
    Bd, T, _ = x_sample.shape
    assert B == 1
    n_pages = page_table.shape[1]
    past_len = n_pages * PAGE
    W = BRANCH_W
    conf_k = conf_dw_w.shape[1]
    sconv_k = sconv_w.shape[1]
    lam_init = 0.8 - 0.6 * math.exp(-0.3 * 0)
    row2 = lambda a: a.reshape(1, -1)

    w_even_bf = w_in_even[0].astype(BF16)
    wo_even_bf = w_out_even[0].astype(BF16)
    wo_odd_bf = w_out_odd[0].astype(BF16)
    wi = w_in_odd[0]
    c_kr = 4 * W + Q_LORA + KV_LORA
    kr_slot = jnp.zeros((D, LANES), F32).at[:, ROPE_LANE0:ROPE_LANE0 + MLA_ROPE].set(
        wi[:, c_kr:c_kr + MLA_ROPE])
    w_odd_bf = jnp.concatenate([wi[:, :c_kr], kr_slot, wi[:, c_kr + MLA_ROPE:]], axis=1).astype(BF16)
    qd = MLA_NOPE + MLA_ROPE
    wuq_bf = jnp.pad(w_uq[0].reshape(Q_LORA, MLA_H, qd),
                     ((0, 0), (0, 0), (0, MLA_QSLOT - qd))).reshape(Q_LORA, -1).astype(BF16)
    wuk_slot_bf = jnp.pad(w_uk[0], ((0, 0), (0, 0), (0, MLA_QSLOT - MLA_NOPE))).reshape(
        KV_LORA, -1).astype(BF16)
    wuk_t_bf = jnp.pad(jnp.transpose(w_uk[0], (1, 2, 0)),
                       ((0, 0), (0, MLA_QSLOT - MLA_NOPE), (0, 0))).astype(BF16)
    wuv_bf = w_uv[0].reshape(KV_LORA, -1).astype(BF16)
    wuv_heads_bf = jnp.transpose(w_uv[0], (1, 0, 2)).astype(BF16)
    lam_params = (row2(lam_q1[0]), row2(lam_k1[0]), row2(lam_q2[0]), row2(lam_k2[0]))
    sg = row2(diff_subln_g[0])

    pos_p = jnp.arange(S)
    cos_d, sin_d = _rope_tables(pos_p, DIFF_DK, 0)
    cos_m, sin_m = _rope_tables(pos_p, MLA_ROPE, ROPE_LANE0)
    tm = _row_tile(S, 256)
    tq = min(S, 512)
    tk = min(S, 512)
    xp = x_prompt[0]
    hist_c = jnp.zeros((32, W), F32)
    ag, zb, qb, kf, kb, vf, vb, st_c = _even_front(
        xp, hist_c, row2(norm_g[0]), w_even_bf, cos_d, sin_d, conf_dw_w[0], row2(conf_dw_b[0]),
        row2(conf_ln_g[0]), row2(conf_ln_b[0]), tm=tm, dil=1, flash_tk=tk)
    bg = _flash(functools.partial(_flash_diff_kernel, lam_init=lam_init), qb, kb, vb, zb,
                (*lam_params, sg.reshape(-1, 1)), dk=LANES, tq=tq, tk=tk, name="flash_diff")
    hist_s = jnp.zeros((8, W), F32)
    x1, cg, zd, ckv_p, kpe_p, st_s, qm, km, vm = _mid(
        xp, ag, bg, hist_s, wo_even_bf, row2(norm_g[1]), w_odd_bf, sconv_w[0],
        row2(mla_q_norm_g[0]), wuq_bf, row2(mla_kv_norm_g[0]), wuk_slot_bf, wuv_bf, cos_m, sin_m,
        tm=tm, dil=1, decode=False, flash_tk=tk)
    dg = _flash(_flash_mla_kernel, qm, km, vm, zd, (), dk=2 * MLA_QSLOT, tq=tq, tk=tk,
                name="flash_mla")
    y_p = _back(x1, cg, dg, wo_odd_bf, row2(final_norm_g), tm=tm)

    y_prompt = y_p[None]
    diff_k_prompt = kf.reshape(1, 1, S, DIFF_H, 2, DIFF_DK)
    diff_v_prompt = vf.reshape(1, 1, S, DIFF_H, DIFF_DV)
    conf_state_prompt = st_c[32 - (conf_k - 1):][None, None]
    mla_ckv_prompt = ckv_p[None, None]
    mla_krope_prompt = kpe_p[:, ROPE_LANE0:ROPE_LANE0 + MLA_ROPE][None, None]
    sconv_state_prompt = st_s[8 - (sconv_k - 1):][None, None]

    R = T * Bd
    tmajor = lambda a: jnp.swapaxes(a, 0, 1).reshape((-1,) + a.shape[2:])
    bmajor = lambda a: jnp.swapaxes(a.reshape((-1, Bd) + a.shape[1:]), 0, 1)
    pos_s = jnp.repeat(past_len + jnp.arange(T), Bd)
    cos_d, sin_d = _rope_tables(pos_s, DIFF_DK, 0)
    cos_m, sin_m = _rope_tables(pos_s, MLA_ROPE, ROPE_LANE0)
    xs = tmajor(x_sample)
    ag, zb, qb, kf, kb, vf, vb, st_c = _even_front(
        xs, tmajor(state_conf[0]), row2(norm_g[0]), w_even_bf, cos_d, sin_d, conf_dw_w[0],
        row2(conf_dw_b[0]), row2(conf_ln_g[0]), row2(conf_ln_b[0]), tm=R, dil=Bd)

    rows_d = 16
    q4 = jnp.transpose(qb.reshape(T, Bd, DIFF_H, LANES), (1, 2, 0, 3))
    lane = jnp.arange(LANES)
    q_maps = jnp.concatenate([jnp.where(lane < DIFF_DK, q4, 0), jnp.where(lane >= DIFF_DK, q4, 0)],
                             axis=2)
    q_dec = jnp.pad(q_maps, ((0, 0), (0, 0), (0, rows_d - 2 * T), (0, 0)))
    new_rows = 16
    pad_new = lambda a: jnp.pad(bmajor(a), ((0, 0), (0, new_rows - T), (0, 0)))
    cache_kt = jnp.transpose(cache_diff_k[0], (0, 2, 3, 4, 1)).reshape(-1, W, PAGE)
    cache_v = cache_diff_v[0].reshape(-1, PAGE * DIFF_H, DIFF_DV)
    o_d = _dec_diff(page_table, q_dec, jnp.swapaxes(pad_new(kf), 1, 2), pad_new(vf),
                    cache_kt, cache_v, pages=min(n_pages, 32), chains=2, t_new=T)
    o_d = o_d.reshape(Bd, DIFF_H, rows_d, DIFF_DV)
    o_maps = jnp.transpose(o_d[:, :, :2 * T].reshape(Bd, DIFF_H, 2, T, DIFF_DV), (2, 3, 0, 1, 4))
    o_maps = o_maps.reshape(2, R, W)
    bg = _diff_post(o_maps[0], o_maps[1], zb, lam_params, sg, lam_init)

    x1, cg, zd, ckv_s, kpe_s, st_s, qa, qr, _ = _mid(
        xs, ag, bg, tmajor(state_sconv[0]), wo_even_bf, row2(norm_g[1]), w_odd_bf, sconv_w[0],
        row2(mla_q_norm_g[0]), wuq_bf, row2(mla_kv_norm_g[0]), wuk_t_bf, wuv_bf, cos_m, sin_m,
        tm=R, dil=Bd, decode=True)
    qa_dec = jnp.transpose(qa.reshape(T, Bd, MLA_H, KV_LORA), (1, 2, 0, 3)).reshape(
        Bd, MLA_H * T, KV_LORA).astype(BF16)
    qp_dec = jnp.transpose(
        qr.reshape(T, Bd, MLA_H, MLA_QSLOT)[..., ROPE_LANE0:ROPE_LANE0 + MLA_ROPE],
        (1, 2, 0, 3)).reshape(Bd, MLA_H * T, MLA_ROPE).astype(BF16)
    kpe_new = kpe_s[:, ROPE_LANE0:ROPE_LANE0 + MLA_ROPE]
    o_m = _dec_mla(page_table, qa_dec, qp_dec, pad_new(ckv_s), jnp.swapaxes(pad_new(kpe_new), 1, 2),
                   cache_mla_ckv[0], jnp.swapaxes(cache_mla_krope[0], 1, 2),
                   pages=min(n_pages, 64), chains=1, t_new=T)
    o_heads = jnp.transpose(o_m.reshape(Bd, MLA_H, T, KV_LORA), (1, 2, 0, 3)).reshape(
        MLA_H, R, KV_LORA)
    dg = _mla_post(o_heads, zd, wuv_heads_bf)
    y_s = _back(x1, cg, dg, wo_odd_bf, row2(final_norm_g), tm=R)

    y_sample = bmajor(y_s)
    diff_k_sample = bmajor(kf).reshape(1, Bd, T, DIFF_H, 2, DIFF_DK)
    diff_v_sample = bmajor(vf).reshape(1, Bd, T, DIFF_H, DIFF_DV)
    keep = lambda prev, new, k: jnp.concatenate([prev[:, new.shape[1]:], new], axis=1)[:, -(k - 1):]
    conf_state_sample = keep(state_conf[0], bmajor(st_c), conf_k)[None]
    mla_ckv_sample = bmajor(ckv_s)[None]
    mla_krope_sample = bmajor(kpe_new)[None]
    sconv_state_sample = keep(state_sconv[0], bmajor(st_s), sconv_k)[None]

    return (y_prompt, y_sample,
            diff_k_prompt, diff_v_prompt, conf_state_prompt,
            mla_ckv_prompt, mla_krope_prompt, sconv_state_prompt,
            diff_k_sample, diff_v_sample, conf_state_sample,
            mla_ckv_sample, mla_krope_sample, sconv_state_sample)
```
